```python
import jax, jax.numpy as jnp
from jax import lax
import numpy as np

D_MODEL = 1024
BATCH = 2
SEQ = 16384
DEPTH = 4

GRID_W = 64
CTX_LEN = 256
HEAD_DIM = 64
N_HEADS = 8
N_KV_HEADS = 2
Q_PER_KV = N_HEADS // N_KV_HEADS
ATTN_WIDTH = N_HEADS * HEAD_DIM
KV_WIDTH = N_KV_HEADS * HEAD_DIM
WINDOW = 128
BLOCK = 128
ROPE_BASE = 10000.0
SGU_GROUPS = 4
SGU_GROUP_DIM = 64
SGU_WIDTH = SGU_GROUPS * SGU_GROUP_DIM
CHUNK = 128
FNET_GROUPS = 4
FNET_GROUP_DIM = 64
FNET_WIDTH = FNET_GROUPS * FNET_GROUP_DIM
N_BRANCHES = 3
IN_SPLITS = [ATTN_WIDTH,
             ATTN_WIDTH + KV_WIDTH,
             ATTN_WIDTH + 2 * KV_WIDTH,
             ATTN_WIDTH + 2 * KV_WIDTH + SGU_WIDTH,
             ATTN_WIDTH + 2 * KV_WIDTH + 2 * SGU_WIDTH,
             ATTN_WIDTH + 2 * KV_WIDTH + 2 * SGU_WIDTH + FNET_WIDTH]
IN_WIDTH = IN_SPLITS[-1] + N_BRANCHES * D_MODEL
N_EXPERTS = 16
CAPACITY_FACTOR = 2
EXPERT_FF = 2048
N_MOD = 6
EPS = 1e-6
NEG_INF = -1e30

kernel_name = "hybrid_gated_branch_diffusion_block"


def rms_norm(x, g):
    xf = x.astype(jnp.float32)
    y = xf * lax.rsqrt(jnp.mean(xf * xf, axis=-1, keepdims=True) + EPS)
    return (y * g.astype(jnp.float32)).astype(x.dtype)


def modulate(h, shift, scale):
    return h * (1 + scale) + shift


def axial_rope(rows):
    row = jnp.repeat(jnp.arange(rows, dtype=jnp.float32), GRID_W)
    col = jnp.tile(jnp.arange(GRID_W, dtype=jnp.float32), rows)
    pairs = HEAD_DIM // 4
    inv = ROPE_BASE ** (-jnp.arange(pairs, dtype=jnp.float32) / pairs)
    ang = jnp.concatenate([row[:, None] * inv, col[:, None] * inv], axis=-1)
    return jnp.cos(ang)[:, None, :], jnp.sin(ang)[:, None, :]


def apply_rope(x, cos, sin):
    cos = cos.astype(x.dtype)
    sin = sin.astype(x.dtype)
    x1, x2 = jnp.split(x, 2, axis=-1)
    return jnp.concatenate([x1 * cos - x2 * sin, x2 * cos + x1 * sin], axis=-1)


def context_attention(q, k, v, sink):
    b, l, _, _ = q.shape
    qg = q.reshape(b, l, N_KV_HEADS, Q_PER_KV, HEAD_DIM).astype(jnp.float32) * (HEAD_DIM ** -0.5)
    s = jnp.einsum('blkgd,bmkd->bkglm', qg, k.astype(jnp.float32))
    sk = jnp.broadcast_to(sink.astype(jnp.float32).reshape(1, N_KV_HEADS, Q_PER_KV, 1, 1), s.shape[:-1] + (1,))
    p = jax.nn.softmax(jnp.concatenate([s, sk], axis=-1), axis=-1)
    o = jnp.einsum('bkglm,bmkd->blkgd', p[..., :l], v.astype(jnp.float32))
    return o.reshape(b, l, ATTN_WIDTH).astype(q.dtype)


def window_attention(q, k, v, k_ctx, v_ctx, sink):
    b, n, _, _ = q.shape
    l = k_ctx.shape[1]
    nb = n // BLOCK
    qb = q.reshape(b, nb, BLOCK, N_KV_HEADS, Q_PER_KV, HEAD_DIM).astype(jnp.float32) * (HEAD_DIM ** -0.5)

    def band(t):
        tp = jnp.pad(t, ((0, 0), (BLOCK, BLOCK), (0, 0), (0, 0))).reshape(b, nb + 2, BLOCK, N_KV_HEADS, HEAD_DIM)
        return jnp.concatenate([tp[:, :-2], tp[:, 1:-1], tp[:, 2:]], axis=2).astype(jnp.float32)

    kb = band(k)
    vb = band(v)
    s_loc = jnp.einsum('bnqkgd,bnskd->bnkgqs', qb, kb)
    qi = jnp.arange(BLOCK)[:, None]
    kj = jnp.arange(3 * BLOCK)[None, :]
    in_win = jnp.abs(kj - BLOCK - qi) <= WINDOW
    key_abs = (jnp.arange(nb)[:, None] - 1) * BLOCK + jnp.arange(3 * BLOCK)[None, :]
    in_range = (key_abs >= 0) & (key_abs < n)
    mask = in_win[None] & in_range[:, None, :]
    s_loc = jnp.where(mask[None, :, None, None], s_loc, NEG_INF)
    s_ctx = jnp.einsum('bnqkgd,bmkd->bnkgqm', qb, k_ctx.astype(jnp.float32))
    sk = jnp.broadcast_to(sink.astype(jnp.float32).reshape(1, 1, N_KV_HEADS, Q_PER_KV, 1, 1), s_ctx.shape[:-1] + (1,))
    p = jax.nn.softmax(jnp.concatenate([s_loc, s_ctx, sk], axis=-1), axis=-1)
    o = (jnp.einsum('bnkgqs,bnskd->bnqkgd', p[..., :3 * BLOCK], vb)
         + jnp.einsum('bnkgqm,bmkd->bnqkgd', p[..., 3 * BLOCK:3 * BLOCK + l], v_ctx.astype(jnp.float32)))
    return o.reshape(b, n, ATTN_WIDTH).astype(q.dtype)


def chunk_sgu(u, v, w_s, b_s):
    b, n, _ = u.shape
    u = jax.nn.gelu(u)
    vg = jax.nn.gelu(v).reshape(b, n // CHUNK, CHUNK, SGU_GROUPS, SGU_GROUP_DIM).astype(jnp.float32)
    vg = vg * lax.rsqrt(jnp.mean(vg * vg, axis=-1, keepdims=True) + EPS)
    z = jnp.einsum('gij,bcjgd->bcigd', w_s.astype(jnp.float32), vg) + b_s.astype(jnp.float32).T[None, None, :, :, None]
    return u * z.reshape(b, n, SGU_WIDTH).astype(u.dtype)


def fourier_mix(f):
    b, n, _ = f.shape
    fg = f.reshape(b, n, FNET_GROUPS, FNET_GROUP_DIM).astype(jnp.float32)
    y = jnp.fft.fft2(fg, axes=(1, 3), norm='ortho').real
    return y.reshape(b, n, FNET_WIDTH).astype(f.dtype)


def merge_branches(a, s, f, gates, w_ba, w_bs, w_bf, w_o):
    g_a, g_s, g_f = jnp.split(jax.nn.sigmoid(gates), N_BRANCHES, axis=-1)
    m = g_a * (a @ w_ba) + g_s * (s @ w_bs) + g_f * (f @ w_bf)
    return m @ w_o


def expert_choice(h, w_router, w_gate, w_up, w_down):
    b, n, d = h.shape
    cap = CAPACITY_FACTOR * n // N_EXPERTS
    aff = jax.nn.softmax((h @ w_router).astype(jnp.float32), axis=-1)
    gval, idx = lax.top_k(jnp.swapaxes(aff, 1, 2), cap)
    xe = jax.vmap(lambda hb, ib: hb[ib])(h, idx)
    a = jnp.einsum('becd,edf->becf', xe, w_gate)
    u = jnp.einsum('becd,edf->becf', xe, w_up)
    y = jnp.einsum('becf,efd->becd', jax.nn.silu(a) * u, w_down) * gval[..., None].astype(h.dtype)
    return jax.vmap(lambda yb, ib: jnp.zeros((n, d), h.dtype).at[ib.reshape(-1)].add(yb.reshape(-1, d)))(y, idx)


def setup_inputs(seed: int = 0) -> dict:
    key = jax.random.key(seed)
    ks = jax.random.split(key, 21)

    def nrm(k, shape, scale):
        return jax.random.normal(k, shape, jnp.float32) * scale

    D = D_MODEL
    return {
        "x": nrm(ks[0], (BATCH, SEQ, D), 1.0),
        "c": nrm(ks[1], (BATCH, D), 1.0),
        "ctx": nrm(ks[2], (BATCH, CTX_LEN, D), 1.0),
        "c_ctx": nrm(ks[3], (D,), 1.0),
        "w_mod": nrm(ks[4], (DEPTH, D, N_MOD * D), 0.5 * D ** -0.5),
        "b_mod": nrm(ks[5], (DEPTH, N_MOD * D), 0.02),
        "g_mix": 1.0 + nrm(ks[6], (DEPTH, D), 0.02),
        "g_ffn": 1.0 + nrm(ks[7], (DEPTH, D), 0.02),
        "w_in": nrm(ks[8], (DEPTH, D, IN_WIDTH), D ** -0.5),
        "attn_sink": nrm(ks[9], (DEPTH, N_HEADS), 0.5),
        "w_spatial": nrm(ks[10], (DEPTH, SGU_GROUPS, CHUNK, CHUNK), CHUNK ** -0.5),
        "b_spatial": nrm(ks[11], (DEPTH, SGU_GROUPS, CHUNK), 0.02),
        "w_branch_attn": nrm(ks[12], (DEPTH, ATTN_WIDTH, D), ATTN_WIDTH ** -0.5),
        "w_branch_sgu": nrm(ks[13], (DEPTH, SGU_WIDTH, D), SGU_WIDTH ** -0.5),
        "w_branch_fourier": nrm(ks[14], (DEPTH, FNET_WIDTH, D), FNET_WIDTH ** -0.5),
        "w_out": nrm(ks[15], (DEPTH, D, D), D ** -0.5),
        "w_router": nrm(ks[16], (DEPTH, D, N_EXPERTS), D ** -0.5),
        "w_gate": nrm(ks[17], (DEPTH, N_EXPERTS, D, EXPERT_FF), D ** -0.5),
        "w_up": nrm(ks[18], (DEPTH, N_EXPERTS, D, EXPERT_FF), D ** -0.5),
        "w_down": nrm(ks[19], (DEPTH, N_EXPERTS, EXPERT_FF, D), EXPERT_FF ** -0.5),
        "g_final": 1.0 + nrm(ks[20], (D,), 0.02),
    }


def reference(x, c, ctx, c_ctx, w_mod, b_mod, g_mix, g_ffn, w_in, attn_sink, w_spatial, b_spatial,
              w_branch_attn, w_branch_sgu, w_branch_fourier, w_out, w_router, w_gate, w_up, w_down, g_final):
    b, n, _ = x.shape
    l_ctx = ctx.shape[1]
    rows = n // GRID_W
    cos, sin = axial_rope(rows)
    for layer in range(DEPTH):
        last = layer == DEPTH - 1
        mod_lat = (jax.nn.silu(c) @ w_mod[layer] + b_mod[layer])[:, None, :]
        mod_ctx = (jax.nn.silu(c_ctx) @ w_mod[layer] + b_mod[layer])[None, None, :]
        sh1, sc1, gt1, sh2, sc2, gt2 = jnp.split(mod_lat, N_MOD, axis=-1)
        csh1, csc1, cgt1, csh2, csc2, cgt2 = jnp.split(mod_ctx, N_MOD, axis=-1)

        h_ctx = modulate(rms_norm(ctx, g_mix[layer]), csh1, csc1)
        if last:
            kv_c = h_ctx @ w_in[layer][:, IN_SPLITS[0]:IN_SPLITS[2]]
            kc, vc = jnp.split(kv_c, 2, axis=-1)
        else:
            qc, kc, vc, uc, vsc, fc, gc = jnp.split(h_ctx @ w_in[layer], IN_SPLITS, axis=-1)
        kc = kc.reshape(b, l_ctx, N_KV_HEADS, HEAD_DIM)
        vc = vc.reshape(b, l_ctx, N_KV_HEADS, HEAD_DIM)
        if not last:
            a_c = context_attention(qc.reshape(b, l_ctx, N_HEADS, HEAD_DIM), kc, vc, attn_sink[layer])
            s_c = chunk_sgu(uc, vsc, w_spatial[layer], b_spatial[layer])
            f_c = fourier_mix(fc)
            ctx_new = ctx + cgt1 * merge_branches(a_c, s_c, f_c, gc, w_branch_attn[layer], w_branch_sgu[layer],
                                                  w_branch_fourier[layer], w_out[layer])
            h2c = modulate(rms_norm(ctx_new, g_ffn[layer]), csh2, csc2)
            ctx_new = ctx_new + cgt2 * expert_choice(h2c, w_router[layer], w_gate[layer], w_up[layer], w_down[layer])

        h = modulate(rms_norm(x, g_mix[layer]), sh1, sc1)
        q, k, v, u, vs, f, gates = jnp.split(h @ w_in[layer], IN_SPLITS, axis=-1)
        q = apply_rope(q.reshape(b, n, N_HEADS, HEAD_DIM), cos, sin)
        k = apply_rope(k.reshape(b, n, N_KV_HEADS, HEAD_DIM), cos, sin)
        v = v.reshape(b, n, N_KV_HEADS, HEAD_DIM)
        a = window_attention(q, k, v, kc, vc, attn_sink[layer])
        s = chunk_sgu(u, vs, w_spatial[layer], b_spatial[layer])
        fo = fourier_mix(f)
        x = x + gt1 * merge_branches(a, s, fo, gates, w_branch_attn[layer], w_branch_sgu[layer],
                                     w_branch_fourier[layer], w_out[layer])
        h2 = modulate(rms_norm(x, g_ffn[layer]), sh2, sc2)
        x = x + gt2 * expert_choice(h2, w_router[layer], w_gate[layer], w_up[layer], w_down[layer])

        if not last:
            ctx = ctx_new
    return rms_norm(x, g_final)
```

```python
import functools
import math

import numpy as np
import jax
import jax.numpy as jnp
from jax import lax
from jax.experimental import pallas as pl
from jax.experimental.pallas import tpu as pltpu

F32 = jnp.float32
BF16 = jnp.bfloat16

D_MODEL = 1024
GRID_W = 64
HEAD_DIM = 64
N_HEADS = 8
N_KV_HEADS = 2
ATTN_WIDTH = N_HEADS * HEAD_DIM
KV_WIDTH = N_KV_HEADS * HEAD_DIM
WINDOW = 128
BLOCK = 128
ROPE_BASE = 10000.0
SGU_GROUPS = 4
SGU_WIDTH = 256
CHUNK = 128
FNET_GROUPS = 4
FNET_GROUP_DIM = 64
FNET_WIDTH = 256
N_EXPERTS = 16
CAPACITY_FACTOR = 2
EXPERT_FF = 2048
N_MOD = 6
EPS = 1e-6
NEG_INF = -1e30
OFF_KV = ATTN_WIDTH
OFF_U = ATTN_WIDTH + 2 * KV_WIDTH
OFF_G = OFF_U + 2 * SGU_WIDTH + FNET_WIDTH
IN_WIDTH = OFF_G + 3 * D_MODEL

LANES = 128
SUBLANES = 8
VMEM_LIMIT_BYTES = 56 * 1024 * 1024

TOKEN_TILE = 512
ATTN_TILE = 256
SEL_ROWS = 128
FF_TILE = 256
ROW_BLOCK = 512
DMA_CHUNK = 256
DMA_UNROLL = 8


def _cparams(sem):
    return pltpu.CompilerParams(dimension_semantics=sem, vmem_limit_bytes=VMEM_LIMIT_BYTES)


def _dot(a, b):
    return jnp.dot(a, b, preferred_element_type=F32)


def _dot_t(a, b):
    return lax.dot_general(a, b, (((1,), (1,)), ((), ())), preferred_element_type=F32)


def _mod_kernel(c_ref, w_ref, b_ref, o_ref):
    c = c_ref[...]
    s = (c * jax.nn.sigmoid(c)).astype(BF16)
    o_ref[0] = _dot(s, w_ref[0].astype(BF16)) + b_ref[0]


def _mod_call(cvec, w_mod, b_mod):
    depth, d, width = w_mod.shape
    tn = 1536
    return pl.pallas_call(
        _mod_kernel,
        grid=(depth, width // tn),
        in_specs=[
            pl.BlockSpec((SUBLANES, d), lambda l, j: (0, 0)),
            pl.BlockSpec((1, d, tn), lambda l, j: (l, 0, j)),
            pl.BlockSpec((1, 1, tn), lambda l, j: (l, 0, j)),
        ],
        out_specs=pl.BlockSpec((1, SUBLANES, tn), lambda l, j: (l, 0, j)),
        out_shape=jax.ShapeDtypeStruct((depth, SUBLANES, width), F32),
        compiler_params=_cparams(("arbitrary", "arbitrary")),
        name="adaln_mod",
    )(cvec, w_mod, b_mod.reshape(depth, 1, width))


def _swap_halves(t):
    lane = lax.broadcasted_iota(jnp.int32, t.shape, 1)
    first = (lane % HEAD_DIM) < (HEAD_DIM // 2)
    return jnp.where(first, pltpu.roll(t, LANES - HEAD_DIM // 2, 1), pltpu.roll(t, HEAD_DIM // 2, 1))


def _dup_heads(t):
    lane = lax.broadcasted_iota(jnp.int32, t.shape, 1)
    lo = lane < HEAD_DIM
    r = pltpu.roll(t, HEAD_DIM, 1)
    return jnp.concatenate([jnp.where(lo, t, r), jnp.where(lo, r, t)], axis=1)


TILE_ROWS = D_MODEL // LANES


def _is_row_layout(shape):
    return shape[-1] == D_MODEL


def _num_tokens(shape):
    return shape[1] if _is_row_layout(shape) else shape[1] // TILE_ROWS


def _load_tokens(ref):
    if _is_row_layout(ref.shape):
        return ref[0]
    tm = ref.shape[1] // TILE_ROWS
    return jnp.concatenate([ref[0, pl.ds(k, tm, stride=TILE_ROWS), :] for k in range(TILE_ROWS)], axis=1)


def _store_tokens(ref, val):
    if _is_row_layout(ref.shape):
        ref[0] = val
    else:
        tm = ref.shape[1] // TILE_ROWS
        for k in range(TILE_ROWS):
            ref[0, pl.ds(k, tm, stride=TILE_ROWS), :] = val[:, k * LANES:(k + 1) * LANES]


def _token_spec(tm, shape):
    rows = tm if _is_row_layout(shape) else tm * TILE_ROWS
    return pl.BlockSpec((1, rows, shape[2]), lambda bi, i: (bi, i, 0))


def _proj_kernel(x_ref, m_ref, g_ref, w_ref, cos_ref, sin_ref,
                 q_ref, k_ref, v_ref, u_ref, vs_ref, f_ref, gt_ref, *, rope):
    x = _load_tokens(x_ref)
    ms = jnp.mean(x * x, axis=-1, keepdims=True)
    y = x * lax.rsqrt(ms + EPS) * g_ref[...]
    h = y * (1.0 + m_ref[0, 1:2, :]) + m_ref[0, 0:1, :]
    hb = h.astype(BF16)

    if rope:
        cos = cos_ref[...]
        sin = sin_ref[...]

    def rot(t):
        if not rope:
            return t
        return t * cos + _swap_halves(t) * sin

    q = _dot(hb, w_ref[:, 0:OFF_KV])
    for j in range(ATTN_WIDTH // LANES):
        qj = rot(q[:, j * LANES:(j + 1) * LANES]) * (HEAD_DIM ** -0.5)
        q_ref[0, :, j * LANES:(j + 1) * LANES] = qj.astype(BF16)
    kv = _dot(hb, w_ref[:, OFF_KV:OFF_U])
    k_ref[0] = _dup_heads(rot(kv[:, 0:KV_WIDTH])).astype(BF16)
    v_ref[0] = _dup_heads(kv[:, KV_WIDTH:2 * KV_WIDTH]).astype(BF16)
    uvf = _dot(hb, w_ref[:, OFF_U:OFF_G])
    u_ref[0] = uvf[:, 0:SGU_WIDTH].astype(BF16)
    vs_ref[0] = uvf[:, SGU_WIDTH:2 * SGU_WIDTH].astype(BF16)
    f_ref[0] = uvf[:, 2 * SGU_WIDTH:].astype(BF16)
    for j in range(3):
        gj = _dot(hb, w_ref[:, OFF_G + j * D_MODEL:OFF_G + (j + 1) * D_MODEL])
        gt_ref[0, :, j * D_MODEL:(j + 1) * D_MODEL] = gj.astype(BF16)


def _proj_call(x, mod, mod_row, g, w, cos_t, sin_t, *, rope):
    b, n = x.shape[0], _num_tokens(x.shape)
    d = D_MODEL
    tm = min(TOKEN_TILE, n)
    if mod_row is None:
        mod_map = lambda bi, i: (bi, 0, 0)
    else:
        mod_map = lambda bi, i: (mod_row, 0, 0)
    widths = [ATTN_WIDTH, 2 * KV_WIDTH, 2 * KV_WIDTH, SGU_WIDTH, SGU_WIDTH, FNET_WIDTH, 3 * D_MODEL]
    return pl.pallas_call(
        functools.partial(_proj_kernel, rope=rope),
        grid=(b, n // tm),
        in_specs=[
            _token_spec(tm, x.shape),
            pl.BlockSpec((1, N_MOD, d), mod_map),
            pl.BlockSpec((1, d), lambda bi, i: (0, 0)),
            pl.BlockSpec((d, IN_WIDTH), lambda bi, i: (0, 0)),
            pl.BlockSpec((tm, LANES), lambda bi, i: (i, 0)),
            pl.BlockSpec((tm, LANES), lambda bi, i: (i, 0)),
        ],
        out_specs=[pl.BlockSpec((1, tm, wd), lambda bi, i: (bi, i, 0)) for wd in widths],
        out_shape=[jax.ShapeDtypeStruct((b, n, wd), BF16) for wd in widths],
        compiler_params=_cparams(("parallel", "arbitrary")),
        name="norm_in_proj",
    )(x, mod, g, w, cos_t, sin_t)


def _attn_kernel(sink_ref, q_ref, *rest, nsub, local):
    if local:
        kp_ref, ko_ref, kn_ref, vp_ref, vo_ref, vn_ref, kc_ref, vc_ref, bias_ref, o_ref = rest
        l_ctx = kc_ref.shape[1]
        i = pl.program_id(1)
        last = pl.num_programs(1) - 1
        kcat = jnp.concatenate([kp_ref[0], ko_ref[0], kn_ref[0]], axis=0)
        vcat = jnp.concatenate([vp_ref[0], vo_ref[0], vn_ref[0]], axis=0)
        bias = bias_ref[...]
        col = lax.broadcasted_iota(jnp.int32, (1, l_ctx + 3 * BLOCK), 1)
        lo_cols = jnp.logical_and(col >= l_ctx, col < l_ctx + BLOCK).astype(F32)
        hi_cols = (col >= l_ctx + 2 * BLOCK).astype(F32)
        edge_lo = jnp.where(i == 0, NEG_INF, 0.0)
        edge_hi = jnp.where(i == last, NEG_INF, 0.0)
    else:
        kc_ref, vc_ref, o_ref = rest
    kc = kc_ref[0]
    vc = vc_ref[0]
    lane_k = lax.broadcasted_iota(jnp.int32, (1, LANES), 1)
    even_k = lane_k < HEAD_DIM
    zero = jnp.zeros((), BF16)
    lane_o = lax.broadcasted_iota(jnp.int32, (BLOCK, LANES), 1)

    for j in range(nsub):
        r0 = j * BLOCK
        pairs = []
        for g in range(N_KV_HEADS):
            c0 = g * 2 * LANES
            lhs = jnp.concatenate([q_ref[0, r0:r0 + BLOCK, c0:c0 + LANES],
                                   q_ref[0, r0:r0 + BLOCK, c0 + LANES:c0 + 2 * LANES]], axis=0)
            keys = kc[:, g * LANES:(g + 1) * LANES]
            vals = vc[:, g * LANES:(g + 1) * LANES]
            if local:
                keys = jnp.concatenate([keys, kcat[r0:r0 + 3 * BLOCK, g * LANES:(g + 1) * LANES]], axis=0)
                vals = jnp.concatenate([vals, vcat[r0:r0 + 3 * BLOCK, g * LANES:(g + 1) * LANES]], axis=0)
                bj = bias
                if j == 0:
                    bj = bj + lo_cols * edge_lo
                if j == nsub - 1:
                    bj = bj + hi_cols * edge_hi
            halves = []
            for par in range(2):
                sel = even_k if par == 0 else jnp.logical_not(even_k)
                s = _dot_t(lhs, jnp.where(sel, keys, zero))
                if local:
                    s = s + bj
                sk = jnp.concatenate([jnp.full((BLOCK, 1), sink_ref[4 * g + par], F32),
                                      jnp.full((BLOCK, 1), sink_ref[4 * g + 2 + par], F32)], axis=0)
                m = jnp.maximum(jnp.max(s, axis=-1, keepdims=True), sk)
                p = jnp.exp(s - m)
                den = jnp.sum(p, axis=-1, keepdims=True) + jnp.exp(sk - m)
                halves.append(_dot(p.astype(BF16), vals) / den)
            for pr in range(2):
                pairs.append(jnp.where(lane_o < HEAD_DIM,
                                       halves[0][pr * BLOCK:(pr + 1) * BLOCK],
                                       halves[1][pr * BLOCK:(pr + 1) * BLOCK]))
        o_ref[0, r0:r0 + BLOCK, :] = jnp.concatenate(pairs, axis=1).astype(BF16)


def _attn_call(q, kd, vd, kcd, vcd, sink, bias, *, local):
    b, n, _ = q.shape
    l_ctx = kcd.shape[1]
    tq = min(ATTN_TILE, n)
    nsub = tq // BLOCK
    nblk = n // BLOCK
    kw = 2 * KV_WIDTH
    in_specs = [pl.BlockSpec(memory_space=pltpu.SMEM),
                pl.BlockSpec((1, tq, ATTN_WIDTH), lambda bi, i: (bi, i, 0))]
    args = [sink, q]
    if local:
        prev_map = lambda bi, i: (bi, jnp.maximum(i * nsub - 1, 0), 0)
        own_map = lambda bi, i: (bi, i, 0)
        next_map = lambda bi, i: (bi, jnp.minimum((i + 1) * nsub, nblk - 1), 0)
        for arr in (kd, vd):
            in_specs += [pl.BlockSpec((1, BLOCK, kw), prev_map),
                         pl.BlockSpec((1, tq, kw), own_map),
                         pl.BlockSpec((1, BLOCK, kw), next_map)]
            args += [arr, arr, arr]
    in_specs += [pl.BlockSpec((1, l_ctx, kw), lambda bi, i: (bi, 0, 0)),
                 pl.BlockSpec((1, l_ctx, kw), lambda bi, i: (bi, 0, 0))]
    args += [kcd, vcd]
    if local:
        in_specs.append(pl.BlockSpec((2 * BLOCK, l_ctx + 3 * BLOCK), lambda bi, i: (0, 0)))
        args.append(bias)
    return pl.pallas_call(
        functools.partial(_attn_kernel, nsub=nsub, local=local),
        grid=(b, n // tq),
        in_specs=in_specs,
        out_specs=pl.BlockSpec((1, tq, ATTN_WIDTH), lambda bi, i: (bi, i, 0)),
        out_shape=jax.ShapeDtypeStruct((b, n, ATTN_WIDTH), BF16),
        compiler_params=_cparams(("parallel", "arbitrary")),
        name="window_attention" if local else "context_attention",
    )(*args)


def _fft1_kernel(w_ref, x_ref, z_ref):
    z_ref[0] = _dot(w_ref[...].astype(BF16), x_ref[0]).astype(BF16)


def _fft2_kernel(t_ref, z_ref, cs_ref, o_ref, *, kb):
    n2 = z_ref.shape[3]
    cs = cs_ref[...].astype(BF16)
    for j in range(kb):
        zk = jnp.concatenate([z_ref[0, 0, j], z_ref[0, 1, j]], axis=0)
        y = _dot(t_ref[j].astype(BF16), zk)
        yc = jnp.concatenate([y[0:n2], y[n2:2 * n2]], axis=1).astype(BF16)
        o_ref[0, :, j, :] = _dot(yc, cs).astype(BF16)


def _fft_dense_kernel(cn_ref, sn_ref, cc_ref, sc_ref, x_ref, o_ref):
    x = x_ref[0]
    a = _dot(x, cc_ref[...].astype(BF16)).astype(BF16)
    bm = _dot(x, sc_ref[...].astype(BF16)).astype(BF16)
    o_ref[0] = (_dot(cn_ref[...].astype(BF16), a) - _dot(sn_ref[...].astype(BF16), bm)).astype(BF16)


def _channel_dft():
    c = np.arange(FNET_GROUP_DIM)
    ang = 2.0 * np.pi * np.outer(c, c) / FNET_GROUP_DIM
    eye = np.eye(FNET_GROUPS)
    cc = np.kron(eye, np.cos(ang)) / math.sqrt(FNET_GROUP_DIM)
    sc = np.kron(eye, np.sin(ang)) / math.sqrt(FNET_GROUP_DIM)
    return cc, sc


@functools.lru_cache(maxsize=None)
def _fft_tables(n):
    n2 = LANES
    n1 = n // n2
    t1 = np.arange(n1)
    a1 = 2.0 * np.pi * np.outer(t1, t1) / n1
    w1 = np.concatenate([np.cos(a1), -np.sin(a1)], axis=0) / math.sqrt(n)
    k = np.arange(n1)[:, None, None] + n1 * np.arange(n2)[None, :, None]
    a2 = 2.0 * np.pi * k * np.arange(n2)[None, None, :] / n
    mr, mi = np.cos(a2), -np.sin(a2)
    tb = np.concatenate([np.concatenate([mr, -mi], axis=2),
                         np.concatenate([mi, mr], axis=2)], axis=1)
    cc, sc = _channel_dft()
    cs = np.concatenate([cc, sc], axis=0)
    return (jnp.asarray(w1, F32), jnp.asarray(tb, F32), jnp.asarray(cs, F32))


@functools.lru_cache(maxsize=None)
def _fft_dense_tables(n):
    t = np.arange(n)
    a = 2.0 * np.pi * np.outer(t, t) / n
    cc, sc = _channel_dft()
    return (jnp.asarray(np.cos(a) / math.sqrt(n), F32), jnp.asarray(np.sin(a) / math.sqrt(n), F32),
            jnp.asarray(cc, F32), jnp.asarray(sc, F32))


def _fourier_call(f):
    b, n, c = f.shape
    if n <= 2 * LANES:
        cn, sn, cc, sc = _fft_dense_tables(n)
        full = lambda bi: (0, 0)
        return pl.pallas_call(
            _fft_dense_kernel,
            grid=(b,),
            in_specs=[pl.BlockSpec((n, n), full), pl.BlockSpec((n, n), full),
                      pl.BlockSpec((c, c), full), pl.BlockSpec((c, c), full),
                      pl.BlockSpec((1, n, c), lambda bi: (bi, 0, 0))],
            out_specs=pl.BlockSpec((1, n, c), lambda bi: (bi, 0, 0)),
            out_shape=jax.ShapeDtypeStruct((b, n, c), BF16),
            compiler_params=_cparams(("parallel",)),
            name="fourier_dense",
        )(cn, sn, cc, sc, f)
    n2 = LANES
    n1 = n // n2
    w1, tb, cs = _fft_tables(n)
    cols = n2 * c
    tc = 2048
    z = pl.pallas_call(
        _fft1_kernel,
        grid=(b, cols // tc),
        in_specs=[pl.BlockSpec((2 * n1, n1), lambda bi, j: (0, 0)),
                  pl.BlockSpec((1, n1, tc), lambda bi, j: (bi, 0, j))],
        out_specs=pl.BlockSpec((1, 2 * n1, tc), lambda bi, j: (bi, 0, j)),
        out_shape=jax.ShapeDtypeStruct((b, 2 * n1, cols), BF16),
        compiler_params=_cparams(("parallel", "arbitrary")),
        name="fourier_stage1",
    )(w1, f.reshape(b, n1, cols))
    kb = SUBLANES
    out = pl.pallas_call(
        functools.partial(_fft2_kernel, kb=kb),
        grid=(n1 // kb, b),
        in_specs=[pl.BlockSpec((kb, 2 * n2, 2 * n2), lambda j, bi: (j, 0, 0)),
                  pl.BlockSpec((1, 2, kb, n2, c), lambda j, bi: (bi, 0, j, 0, 0)),
                  pl.BlockSpec((2 * c, c), lambda j, bi: (0, 0))],
        out_specs=pl.BlockSpec((1, n2, kb, c), lambda j, bi: (bi, 0, j, 0)),
        out_shape=jax.ShapeDtypeStruct((b, n2, n1, c), BF16),
        compiler_params=_cparams(("arbitrary", "arbitrary")),
        name="fourier_stage2",
    )(tb, z.reshape(b, 2, n1, n2, c), cs)
    return out.reshape(b, n, c)


def _merge_kernel(a_ref, u_ref, vs_ref, fo_ref, gt_ref, x_ref, m_ref, g2_ref,
                  wsp_ref, bsp_ref, bd_ref, wba_ref, wbs_ref, wbf_ref, wo_ref, wr_ref,
                  xo_ref, h2_ref, aff_ref, s_scr):
    tm = a_ref.shape[1]
    lane = lax.broadcasted_iota(jnp.int32, (CHUNK, SGU_WIDTH), 1)
    for c in range(tm // CHUNK):
        r0 = c * CHUNK
        ug = jax.nn.gelu(u_ref[0, r0:r0 + CHUNK, :].astype(F32))
        vg = jax.nn.gelu(vs_ref[0, r0:r0 + CHUNK, :].astype(F32))
        sq = vg * vg
        sq_hi = sq.astype(BF16)
        sq_lo = (sq - sq_hi.astype(F32)).astype(BF16)
        msq = _dot(sq_hi, bd_ref[...]) + _dot(sq_lo, bd_ref[...])
        vn = (vg * lax.rsqrt(msq + EPS)).astype(BF16)
        zero = jnp.zeros((), BF16)
        stack = jnp.concatenate(
            [jnp.where(lax.shift_right_logical(lane, 6) == g, vn, zero) for g in range(SGU_GROUPS)], axis=0)
        z = _dot(wsp_ref[...], stack) + bsp_ref[...]
        s_scr[r0:r0 + CHUNK, :] = (ug * z).astype(BF16)

    gates = gt_ref[0]
    acc = jax.nn.sigmoid(gates[:, 0:D_MODEL].astype(F32)) * _dot(a_ref[0], wba_ref[...])
    acc = acc + jax.nn.sigmoid(gates[:, D_MODEL:2 * D_MODEL].astype(F32)) * _dot(s_scr[...], wbs_ref[...])
    acc = acc + jax.nn.sigmoid(gates[:, 2 * D_MODEL:].astype(F32)) * _dot(fo_ref[0], wbf_ref[...])
    o = _dot(acc.astype(BF16), wo_ref[...])
    xn = _load_tokens(x_ref) + m_ref[0, 2:3, :] * o
    _store_tokens(xo_ref, xn)
    ms = jnp.mean(xn * xn, axis=-1, keepdims=True)
    y = xn * lax.rsqrt(ms + EPS) * g2_ref[...]
    h2 = y * (1.0 + m_ref[0, 4:5, :]) + m_ref[0, 3:4, :]
    _store_tokens(h2_ref, h2)
    logits = _dot_t(wr_ref[...], h2.astype(BF16))
    mx = jnp.max(logits, axis=0, keepdims=True)
    ex = jnp.exp(logits - mx)
    aff_ref[0] = ex / jnp.sum(ex, axis=0, keepdims=True)


def _merge_call(a, u, vs, fo, gates, x, mod, mod_row, g2, wsp, bsp, bd, wba, wbs, wbf, wo, wr_t):
    b, n = x.shape[0], _num_tokens(x.shape)
    d = D_MODEL
    tm = min(TOKEN_TILE, n)
    if mod_row is None:
        mod_map = lambda bi, i: (bi, 0, 0)
    else:
        mod_map = lambda bi, i: (mod_row, 0, 0)
    tok = lambda wd: pl.BlockSpec((1, tm, wd), lambda bi, i: (bi, i, 0))
    full = lambda arr: pl.BlockSpec(arr.shape, lambda bi, i: (0,) * arr.ndim)
    tiled = (b, n * TILE_ROWS, LANES)
    return pl.pallas_call(
        _merge_kernel,
        grid=(b, n // tm),
        in_specs=[tok(ATTN_WIDTH), tok(SGU_WIDTH), tok(SGU_WIDTH), tok(FNET_WIDTH), tok(3 * D_MODEL),
                  _token_spec(tm, x.shape),
                  pl.BlockSpec((1, N_MOD, d), mod_map), full(g2),
                  full(wsp), full(bsp), full(bd), full(wba), full(wbs), full(wbf), full(wo), full(wr_t)],
        out_specs=[_token_spec(tm, tiled), _token_spec(tm, tiled),
                   pl.BlockSpec((1, N_EXPERTS, tm), lambda bi, i: (bi, 0, i))],
        out_shape=[jax.ShapeDtypeStruct(tiled, F32), jax.ShapeDtypeStruct(tiled, F32),
                   jax.ShapeDtypeStruct((b, N_EXPERTS, n), F32)],
        scratch_shapes=[pltpu.VMEM((tm, SGU_WIDTH), BF16)],
        compiler_params=_cparams(("parallel", "arbitrary")),
        name="merge_norm_router",
    )(a, u, vs, fo, gates, x, mod, g2, wsp, bsp, bd, wba, wbs, wbf, wo, wr_t)


def _select_kernel(aff_ref, tri_ref, lstrict_ref, idx_ref, *, cap, slots):
    rows = SEL_ROWS
    ones = jnp.ones((LANES, LANES), BF16)
    tri = tri_ref[...]
    lstrict = lstrict_ref[...]
    lane_r = lax.broadcasted_iota(jnp.int32, (rows, LANES), 1)
    row_r = lax.broadcasted_iota(jnp.int32, (rows, LANES), 0)
    eye = lane_r == row_r

    def total(mask_i32):
        return jnp.sum(jnp.sum(mask_i32, axis=0, keepdims=True), axis=1, keepdims=True)

    def cumsum(mask):
        mb = jnp.where(mask, 1.0, 0.0).astype(BF16)
        within = _dot(mb, tri)
        tot = jnp.broadcast_to(within[:, LANES - 1:LANES], (rows, LANES)).astype(BF16)
        return within + _dot(lstrict, tot)

    def split(v):
        hi = jnp.floor(v * (1.0 / 256.0))
        return hi.astype(BF16), (v - hi * 256.0).astype(BF16)

    aff_all = [aff_ref[0, e] for e in range(N_EXPERTS)]

    def as_f32(bits):
        return lax.bitcast_convert_type(bits, F32)

    def bit_step(t, prefixes):
        bit = lax.shift_left(jnp.int32(1), 30 - t)
        out = []
        for e in range(N_EXPERTS):
            cand = prefixes[e] | bit
            cnt = total((aff_all[e] >= as_f32(cand)).astype(jnp.int32))
            out.append(jnp.where(cnt >= cap, cand, prefixes[e]))
        return tuple(out)

    thr = lax.fori_loop(0, 31, bit_step, tuple(jnp.zeros((1, 1), jnp.int32) for _ in range(N_EXPERTS)))

    slot = lax.broadcasted_iota(jnp.int32, (slots, LANES), 0).astype(F32)
    lane_s = lax.broadcasted_iota(jnp.int32, (slots, LANES), 1).astype(F32)
    for e in range(N_EXPERTS):
        gt = aff_all[e] >= as_f32(thr[e] + 1)
        eq = jnp.logical_and(aff_all[e] >= as_f32(thr[e]), jnp.logical_not(gt))
        need = (cap - total(gt.astype(jnp.int32))).astype(F32)
        eq_rank = cumsum(eq) - jnp.where(eq, 1.0, 0.0)
        sel = jnp.logical_or(gt, jnp.logical_and(eq, eq_rank < need))
        gcum = cumsum(sel)
        ends = jnp.where(eye, jnp.broadcast_to(gcum[:, LANES - 1:LANES], (rows, LANES)), 0.0)
        e_hi, e_lo = split(ends)
        ones_s = jnp.ones((slots, rows), BF16)
        ends_row = _dot(ones_s, e_hi) * 256.0 + _dot(ones_s, e_lo)
        row_of = _dot(jnp.where(ends_row <= slot, 1.0, 0.0).astype(BF16), ones)
        onehot = jnp.where(lane_s == row_of, 1.0, 0.0).astype(BF16)
        g_hi, g_lo = split(gcum)
        grow = _dot(onehot, g_hi) * 256.0 + _dot(onehot, g_lo)
        lane_of = _dot(jnp.where(grow <= slot, 1.0, 0.0).astype(BF16), ones)
        token = (row_of * float(LANES) + lane_of).astype(jnp.int32)
        for blk in range(slots // LANES):
            tb = token[blk * LANES:(blk + 1) * LANES, :]
            idx_ref[0, e, blk:blk + 1, :] = jnp.sum(jnp.where(eye, tb, 0), axis=0, keepdims=True)


def _select_call(aff, cap):
    b, e, n = aff.shape
    full = SEL_ROWS * LANES
    if n < full:
        aff = jnp.pad(aff, ((0, 0), (0, 0), (0, full - n)), constant_values=-1.0)
    slots = max(LANES, cap)
    r = np.arange(LANES)
    tri = jnp.asarray(r[:, None] <= r[None, :], BF16)
    lstrict = jnp.asarray(r[None, :] < r[:, None], BF16)
    idx = pl.pallas_call(
        functools.partial(_select_kernel, cap=cap, slots=slots),
        grid=(b,),
        in_specs=[pl.BlockSpec((1, e, SEL_ROWS, LANES), lambda bi: (bi, 0, 0, 0)),
                  pl.BlockSpec((LANES, LANES), lambda bi: (0, 0)),
                  pl.BlockSpec((LANES, LANES), lambda bi: (0, 0))],
        out_specs=pl.BlockSpec((1, e, slots // LANES, LANES), lambda bi: (bi, 0, 0, 0)),
        out_shape=jax.ShapeDtypeStruct((b, e, slots // LANES, LANES), jnp.int32),
        compiler_params=_cparams(("parallel",)),
        name="expert_choice_select",
    )(aff.reshape(b, e, SEL_ROWS, LANES), tri, lstrict)
    return idx.reshape(b, e, slots)[:, :, :cap]


def _expert_kernel(idx_ref, *refs, chunks, nsrc, rblk):
    h2_hbm = refs[0:nsrc]
    wg_ref, wu_ref, wd_ref, wr_ref, gt2_ref = refs[2 * nsrc:2 * nsrc + 5]
    x_hbm = refs[2 * nsrc + 5:3 * nsrc + 5]
    xe, acc, stage, sem = refs[3 * nsrc + 5:]
    e = pl.program_id(0)
    f = pl.program_id(1)
    nf = pl.num_programs(1)
    nchunk = len(chunks)
    rows = xe.shape[0]

    def wave(hbms, ch, slot, to_vmem):
        src_id, r0, nrows, _ = chunks[ch]
        hbm = hbms[src_id]

        def body(t, carry):
            for s in range(DMA_UNROLL):
                r = t * DMA_UNROLL + s
                tok = idx_ref[0, 0, r0 + r]
                h_tile = hbm.at[pl.ds(pl.multiple_of(tok * TILE_ROWS, TILE_ROWS), TILE_ROWS), :]
                v_tile = stage.at[slot, pl.ds(pl.multiple_of(r * TILE_ROWS, TILE_ROWS), TILE_ROWS), :]
                if to_vmem:
                    pltpu.make_async_copy(h_tile, v_tile, sem.at[slot]).start()
                else:
                    pltpu.make_async_copy(v_tile, h_tile, sem.at[slot]).start()
            return carry
        lax.fori_loop(0, nrows // DMA_UNROLL, body, 0)

    def wave_wait(hbms, ch, slot):
        src_id, _, nrows, _ = chunks[ch]
        pltpu.make_async_copy(hbms[src_id].at[pl.ds(0, nrows * TILE_ROWS), :],
                              stage.at[slot, pl.ds(0, nrows * TILE_ROWS), :], sem.at[slot]).wait()

    def tile_rows(nrows, k):
        return pl.ds(k, nrows, stride=TILE_ROWS)

    @pl.when(f == 0)
    def _gather():
        wave(h2_hbm, 0, 0, True)
        for ch in range(nchunk):
            slot = ch % 2
            _, r0, nrows, _ = chunks[ch]
            if ch + 1 < nchunk:
                wave(h2_hbm, ch + 1, 1 - slot, True)
            wave_wait(h2_hbm, ch, slot)
            for k in range(TILE_ROWS):
                xe[r0:r0 + nrows, k * LANES:(k + 1) * LANES] = stage[slot, tile_rows(nrows, k), :].astype(BF16)
        acc[...] = jnp.zeros_like(acc)

    wg = wg_ref[0, 0].astype(BF16)
    wu = wu_ref[0, 0].astype(BF16)
    wd = wd_ref[0, 0].astype(BF16)

    def ffn_rows(r0, nrows):
        xs = xe[pl.ds(r0, nrows), :]
        a = _dot(xs, wg)
        u = _dot(xs, wu)
        hm = (a * jax.nn.sigmoid(a) * u).astype(BF16)
        acc[pl.ds(r0, nrows), :] += _dot(hm, wd)

    def ffn_block(rb, carry):
        ffn_rows(pl.multiple_of(rb * rblk, rblk), rblk)
        return carry

    lax.fori_loop(0, rows // rblk, ffn_block, 0)
    if rows % rblk:
        ffn_rows((rows // rblk) * rblk, rows % rblk)

    @pl.when(f == nf - 1)
    def _combine():
        wave(x_hbm, 0, 0, True)
        for ch in range(nchunk):
            slot = ch % 2
            _, r0, nrows, gt_row = chunks[ch]
            if ch + 1 < nchunk:
                if ch >= 1:
                    wave_wait(x_hbm, ch - 1, 1 - slot)
                wave(x_hbm, ch + 1, 1 - slot, True)
            wave_wait(x_hbm, ch, slot)
            lane = lax.broadcasted_iota(jnp.int32, (nrows, LANES), 1)
            logits = _dot(xe[r0:r0 + nrows, :], wr_ref[...])
            logits = jnp.where(lane < N_EXPERTS, logits, NEG_INF)
            ex = jnp.exp(logits - jnp.max(logits, axis=-1, keepdims=True))
            gval = (jnp.sum(jnp.where(lane == e, ex, 0.0), axis=-1, keepdims=True)
                    / jnp.sum(ex, axis=-1, keepdims=True))
            for k in range(TILE_ROWS):
                cols = slice(k * LANES, (k + 1) * LANES)
                stage[slot, tile_rows(nrows, k), :] = (
                    stage[slot, tile_rows(nrows, k), :]
                    + gt2_ref[gt_row:gt_row + 1, cols] * (acc[r0:r0 + nrows, cols] * gval))
            wave(x_hbm, ch, slot, False)
        wave_wait(x_hbm, nchunk - 1, (nchunk - 1) % 2)
        if nchunk >= 2:
            wave_wait(x_hbm, nchunk - 2, nchunk % 2)


def _expert_choice(groups, layer, w_gate, w_up, w_down, wr_pad, gt2):
    d = D_MODEL
    ff = w_gate.shape[-1]
    tf = FF_TILE
    idx_parts, chunks, h2s, xs = [], [], [], []
    row0 = 0
    for src, (h2, aff, x, gt_rows) in enumerate(groups):
        b, n = x.shape[0], _num_tokens(x.shape)
        cap = CAPACITY_FACTOR * n // N_EXPERTS
        idx = _select_call(aff, cap)
        flat = idx + (jnp.arange(b, dtype=jnp.int32) * n)[:, None, None]
        idx_parts.append(jnp.transpose(flat, (1, 0, 2)).reshape(N_EXPERTS, b * cap))
        if cap >= DMA_CHUNK:
            for bi in range(b):
                for c0 in range(0, cap, DMA_CHUNK):
                    chunks.append((src, row0 + bi * cap + c0, DMA_CHUNK, gt_rows[bi]))
        else:
            assert all(r == gt_rows[0] for r in gt_rows) and b * cap <= DMA_CHUNK
            chunks.append((src, row0, b * cap, gt_rows[0]))
        row0 += b * cap
        h2s.append(h2.reshape(b * n * TILE_ROWS, LANES))
        xs.append(x.reshape(b * n * TILE_ROWS, LANES))
    rows = row0
    nsrc = len(groups)
    idx_all = jnp.concatenate(idx_parts, axis=1).reshape(N_EXPERTS, 1, rows)
    rblk = min(ROW_BLOCK, rows)
    any_spec = pl.BlockSpec(memory_space=pl.ANY)
    outs = pl.pallas_call(
        functools.partial(_expert_kernel, chunks=tuple(chunks), nsrc=nsrc, rblk=rblk),
        grid=(N_EXPERTS, ff // tf),
        in_specs=[pl.BlockSpec((1, 1, rows), lambda ei, fi: (ei, 0, 0), memory_space=pltpu.SMEM)]
        + [any_spec] * (2 * nsrc)
        + [pl.BlockSpec((1, 1, d, tf), lambda ei, fi: (layer, ei, 0, fi)),
           pl.BlockSpec((1, 1, d, tf), lambda ei, fi: (layer, ei, 0, fi)),
           pl.BlockSpec((1, 1, tf, d), lambda ei, fi: (layer, ei, fi, 0)),
           pl.BlockSpec((d, LANES), lambda ei, fi: (0, 0)),
           pl.BlockSpec(gt2.shape, lambda ei, fi: (0, 0))],
        out_specs=[any_spec] * nsrc,
        out_shape=[jax.ShapeDtypeStruct(xa.shape, F32) for xa in xs],
        scratch_shapes=[pltpu.VMEM((rows, d), BF16), pltpu.VMEM((rows, d), F32),
                        pltpu.VMEM((2, DMA_CHUNK * TILE_ROWS, LANES), F32), pltpu.SemaphoreType.DMA((2,))],
        input_output_aliases={1 + nsrc + s: s for s in range(nsrc)},
        compiler_params=_cparams(("arbitrary", "arbitrary")),
        name="expert_ffn",
    )(idx_all, *h2s, *xs, w_gate, w_up, w_down, wr_pad, gt2)
    return [o.reshape(g[2].shape) for o, g in zip(outs, groups)]


def _final_norm_kernel(x_ref, g_ref, o_ref):
    x = _load_tokens(x_ref)
    ms = jnp.mean(x * x, axis=-1, keepdims=True)
    o_ref[0] = x * lax.rsqrt(ms + EPS) * g_ref[...]


def _final_norm_call(x, g):
    b, n = x.shape[0], _num_tokens(x.shape)
    d = D_MODEL
    tm = min(TOKEN_TILE, n)
    return pl.pallas_call(
        _final_norm_kernel,
        grid=(b, n // tm),
        in_specs=[_token_spec(tm, x.shape),
                  pl.BlockSpec((1, d), lambda bi, i: (0, 0))],
        out_specs=pl.BlockSpec((1, tm, d), lambda bi, i: (bi, i, 0)),
        out_shape=jax.ShapeDtypeStruct((b, n, d), F32),
        compiler_params=_cparams(("parallel", "arbitrary")),
        name="final_norm",
    )(x, g)


def _rope_tables(n):
    rows = n // GRID_W
    row = np.repeat(np.arange(rows, dtype=np.float32), GRID_W)
    col = np.tile(np.arange(GRID_W, dtype=np.float32), rows)
    pairs = HEAD_DIM // 4
    inv = (np.float32(ROPE_BASE) ** (-np.arange(pairs, dtype=np.float32) / pairs)).astype(np.float32)
    ang = np.concatenate([row[:, None] * inv, col[:, None] * inv], axis=-1).astype(np.float32)
    cos, sin = np.cos(ang), np.sin(ang)
    cos_t = np.concatenate([cos, cos, cos, cos], axis=-1)
    sin_t = np.concatenate([-sin, sin, -sin, sin], axis=-1)
    return jnp.asarray(cos_t, F32), jnp.asarray(sin_t, F32)


def _band_bias(l_ctx):
    qi = np.arange(BLOCK)[:, None]
    kj = np.arange(3 * BLOCK)[None, :]
    ok = np.abs(kj - BLOCK - qi) <= WINDOW
    bias = np.concatenate([np.zeros((BLOCK, l_ctx)), np.where(ok, 0.0, NEG_INF)], axis=1).astype(np.float32)
    return jnp.asarray(np.concatenate([bias, bias], axis=0))


def kernel(x, c, ctx, c_ctx, w_mod, b_mod, g_mix, g_ffn, w_in, attn_sink, w_spatial, b_spatial,
           w_branch_attn, w_branch_sgu, w_branch_fourier, w_out, w_router, w_gate, w_up, w_down, g_final):
    b, n, d = x.shape
    l_ctx = ctx.shape[1]
    depth = w_mod.shape[0]

    cvec = jnp.zeros((SUBLANES, d), F32).at[0:b].set(c).at[b].set(c_ctx)
    mod = _mod_call(cvec, w_mod, b_mod).reshape(depth, SUBLANES, N_MOD, d)
    ctx_row = b

    cos_t, sin_t = _rope_tables(n)
    bias = _band_bias(l_ctx)
    no_rope = jnp.zeros((l_ctx, LANES), F32)
    gsz = SGU_WIDTH // SGU_GROUPS
    bd = jnp.asarray(np.kron(np.eye(SGU_GROUPS), np.ones((gsz, gsz))) / gsz, BF16)

    for layer in range(depth):
        last = layer == depth - 1
        w_l = w_in[layer].astype(BF16)
        g1 = g_mix[layer].reshape(1, d)
        g2 = g_ffn[layer].reshape(1, d)
        wsp = jnp.transpose(w_spatial[layer], (1, 0, 2)).reshape(CHUNK, SGU_GROUPS * CHUNK).astype(BF16)
        bsp = jnp.repeat(jnp.transpose(b_spatial[layer]), gsz, axis=1)
        wba = w_branch_attn[layer].astype(BF16)
        wbs = w_branch_sgu[layer].astype(BF16)
        wbf = w_branch_fourier[layer].astype(BF16)
        wo = w_out[layer].astype(BF16)
        wr_t = jnp.transpose(w_router[layer]).astype(BF16)
        wr_pad = jnp.pad(w_router[layer], ((0, 0), (0, LANES - N_EXPERTS))).astype(BF16)
        sink = attn_sink[layer]
        mod_l = mod[layer]

        qc, kcd, vcd, uc, vsc, fc, gc = _proj_call(ctx, mod_l, ctx_row, g1, w_l, no_rope, no_rope, rope=False)
        if not last:
            a_c = _attn_call(qc, None, None, kcd, vcd, sink, None, local=False)
            f_c = _fourier_call(fc)
            ctx_mid, h2c, aff_c = _merge_call(a_c, uc, vsc, f_c, gc, ctx, mod_l, ctx_row, g2,
                                              wsp, bsp, bd, wba, wbs, wbf, wo, wr_t)

        q, kd, vd, u, vs, f, gates = _proj_call(x, mod_l, None, g1, w_l, cos_t, sin_t, rope=True)
        a = _attn_call(q, kd, vd, kcd, vcd, sink, bias, local=True)
        fo = _fourier_call(f)
        x_mid, h2, aff = _merge_call(a, u, vs, fo, gates, x, mod_l, None, g2,
                                     wsp, bsp, bd, wba, wbs, wbf, wo, wr_t)

        gt2 = mod_l[:, 5]
        groups = [(h2, aff, x_mid, list(range(b)))]
        if not last:
            groups.append((h2c, aff_c, ctx_mid, [ctx_row] * b))
        outs = _expert_choice(groups, layer, w_gate, w_up, w_down, wr_pad, gt2)
        x = outs[0]
        if not last:
            ctx = outs[1]
    return _final_norm_call(x, g_final.reshape(1, d))
```

```python
import functools
import math

import numpy as np
import jax
import jax.numpy as jnp
from jax import lax
from jax.experimental import pallas as pl
from jax.experimental.pallas import tpu as pltpu

F32 = jnp.float32
BF16 = jnp.bfloat16

D_MODEL = 1024
GRID_W = 64
HEAD_DIM = 64
N_HEADS = 8
N_KV_HEADS = 2
ATTN_WIDTH = N_HEADS * HEAD_DIM
KV_WIDTH = N_KV_HEADS * HEAD_DIM
WINDOW = 128
BLOCK = 128
ROPE_BASE = 10000.0
SGU_GROUPS = 4
SGU_WIDTH = 256
CHUNK = 128
FNET_GROUPS = 4
FNET_GROUP_DIM = 64
FNET_WIDTH = 256
N_EXPERTS = 16
CAPACITY_FACTOR = 2
EXPERT_FF = 2048
N_MOD = 6
EPS = 1e-6
NEG_INF = -1e30
OFF_KV = ATTN_WIDTH
OFF_U = ATTN_WIDTH + 2 * KV_WIDTH
OFF_G = OFF_U + 2 * SGU_WIDTH + FNET_WIDTH
IN_WIDTH = OFF_G + 3 * D_MODEL

LANES = 128
SUBLANES = 8
VMEM_LIMIT_BYTES = 56 * 1024 * 1024

TOKEN_TILE = 512
ATTN_TILE = 256
SEL_ROWS = 128
FF_TILE = 256
ROW_BLOCK = 512
DMA_UNROLL = 8
X_SLOTS = 3


def _cparams(sem):
    return pltpu.CompilerParams(dimension_semantics=sem, vmem_limit_bytes=VMEM_LIMIT_BYTES)


def _dot(a, b):
    return jnp.dot(a, b, preferred_element_type=F32)


def _dot_t(a, b):
    return lax.dot_general(a, b, (((1,), (1,)), ((), ())), preferred_element_type=F32)


def _mod_kernel(c_ref, w_ref, b_ref, o_ref):
    c = c_ref[...]
    s = (c * jax.nn.sigmoid(c)).astype(BF16)
    o_ref[0] = _dot(s, w_ref[0].astype(BF16)) + b_ref[0]


def _mod_call(cvec, w_mod, b_mod):
    depth, d, width = w_mod.shape
    tn = 1536
    return pl.pallas_call(
        _mod_kernel,
        grid=(depth, width // tn),
        in_specs=[
            pl.BlockSpec((SUBLANES, d), lambda l, j: (0, 0)),
            pl.BlockSpec((1, d, tn), lambda l, j: (l, 0, j)),
            pl.BlockSpec((1, 1, tn), lambda l, j: (l, 0, j)),
        ],
        out_specs=pl.BlockSpec((1, SUBLANES, tn), lambda l, j: (l, 0, j)),
        out_shape=jax.ShapeDtypeStruct((depth, SUBLANES, width), F32),
        compiler_params=_cparams(("arbitrary", "arbitrary")),
        name="adaln_mod",
    )(cvec, w_mod, b_mod.reshape(depth, 1, width))


def _swap_halves(t):
    lane = lax.broadcasted_iota(jnp.int32, t.shape, 1)
    first = (lane % HEAD_DIM) < (HEAD_DIM // 2)
    return jnp.where(first, pltpu.roll(t, LANES - HEAD_DIM // 2, 1), pltpu.roll(t, HEAD_DIM // 2, 1))


def _dup_heads(t):
    lane = lax.broadcasted_iota(jnp.int32, t.shape, 1)
    lo = lane < HEAD_DIM
    r = pltpu.roll(t, HEAD_DIM, 1)
    return jnp.concatenate([jnp.where(lo, t, r), jnp.where(lo, r, t)], axis=1)


TILE_ROWS = D_MODEL // LANES


def _is_row_layout(shape):
    return shape[-1] == D_MODEL


def _num_tokens(shape):
    return shape[1] if _is_row_layout(shape) else shape[1] // TILE_ROWS


def _load_tokens(ref):
    if _is_row_layout(ref.shape):
        return ref[0]
    tm = ref.shape[1] // TILE_ROWS
    return jnp.concatenate([ref[0, pl.ds(k, tm, stride=TILE_ROWS), :] for k in range(TILE_ROWS)], axis=1)


def _store_tokens(ref, val):
    if _is_row_layout(ref.shape):
        ref[0] = val
    else:
        tm = ref.shape[1] // TILE_ROWS
        for k in range(TILE_ROWS):
            ref[0, pl.ds(k, tm, stride=TILE_ROWS), :] = val[:, k * LANES:(k + 1) * LANES]


def _token_spec(tm, shape):
    rows = tm if _is_row_layout(shape) else tm * TILE_ROWS
    return pl.BlockSpec((1, rows, shape[2]), lambda bi, i: (bi, i, 0))


def _proj_kernel(x_ref, m_ref, g_ref, w_ref, cos_ref, sin_ref,
                 q_ref, k_ref, v_ref, u_ref, vs_ref, f_ref, gt_ref, *, rope):
    x = _load_tokens(x_ref)
    ms = jnp.mean(x * x, axis=-1, keepdims=True)
    y = x * lax.rsqrt(ms + EPS) * g_ref[...]
    h = y * (1.0 + m_ref[0, 1:2, :]) + m_ref[0, 0:1, :]
    hb = h.astype(BF16)

    if rope:
        cos = cos_ref[...]
        sin = sin_ref[...]

    def rot(t):
        if not rope:
            return t
        return t * cos + _swap_halves(t) * sin

    q = _dot(hb, w_ref[:, 0:OFF_KV])
    for j in range(ATTN_WIDTH // LANES):
        qj = rot(q[:, j * LANES:(j + 1) * LANES]) * (HEAD_DIM ** -0.5)
        q_ref[0, :, j * LANES:(j + 1) * LANES] = qj.astype(BF16)
    kv = _dot(hb, w_ref[:, OFF_KV:OFF_U])
    k_ref[0] = _dup_heads(rot(kv[:, 0:KV_WIDTH])).astype(BF16)
    v_ref[0] = _dup_heads(kv[:, KV_WIDTH:2 * KV_WIDTH]).astype(BF16)
    uvf = _dot(hb, w_ref[:, OFF_U:OFF_G])
    u_ref[0] = uvf[:, 0:SGU_WIDTH].astype(BF16)
    vs_ref[0] = uvf[:, SGU_WIDTH:2 * SGU_WIDTH].astype(BF16)
    f_ref[0] = uvf[:, 2 * SGU_WIDTH:].astype(BF16)
    for j in range(3):
        gj = _dot(hb, w_ref[:, OFF_G + j * D_MODEL:OFF_G + (j + 1) * D_MODEL])
        gt_ref[0, :, j * D_MODEL:(j + 1) * D_MODEL] = gj.astype(BF16)


def _proj_call(x, mod, mod_row, g, w, cos_t, sin_t, *, rope):
    b, n = x.shape[0], _num_tokens(x.shape)
    d = D_MODEL
    tm = min(TOKEN_TILE, n)
    if mod_row is None:
        mod_map = lambda bi, i: (bi, 0, 0)
    else:
        mod_map = lambda bi, i: (mod_row, 0, 0)
    widths = [ATTN_WIDTH, 2 * KV_WIDTH, 2 * KV_WIDTH, SGU_WIDTH, SGU_WIDTH, FNET_WIDTH, 3 * D_MODEL]
    return pl.pallas_call(
        functools.partial(_proj_kernel, rope=rope),
        grid=(b, n // tm),
        in_specs=[
            _token_spec(tm, x.shape),
            pl.BlockSpec((1, N_MOD, d), mod_map),
            pl.BlockSpec((1, d), lambda bi, i: (0, 0)),
            pl.BlockSpec((d, IN_WIDTH), lambda bi, i: (0, 0)),
            pl.BlockSpec((tm, LANES), lambda bi, i: (i, 0)),
            pl.BlockSpec((tm, LANES), lambda bi, i: (i, 0)),
        ],
        out_specs=[pl.BlockSpec((1, tm, wd), lambda bi, i: (bi, i, 0)) for wd in widths],
        out_shape=[jax.ShapeDtypeStruct((b, n, wd), BF16) for wd in widths],
        compiler_params=_cparams(("parallel", "arbitrary")),
        name="norm_in_proj",
    )(x, mod, g, w, cos_t, sin_t)


def _attn_kernel(sink_ref, q_ref, *rest, nsub, local):
    if local:
        kp_ref, ko_ref, kn_ref, vp_ref, vo_ref, vn_ref, kc_ref, vc_ref, bias_ref, o_ref = rest
        l_ctx = kc_ref.shape[1]
        i = pl.program_id(1)
        last = pl.num_programs(1) - 1
        kcat = jnp.concatenate([kp_ref[0], ko_ref[0], kn_ref[0]], axis=0)
        vcat = jnp.concatenate([vp_ref[0], vo_ref[0], vn_ref[0]], axis=0)
        bias = bias_ref[...]
        col = lax.broadcasted_iota(jnp.int32, (1, l_ctx + 3 * BLOCK), 1)
        lo_cols = jnp.logical_and(col >= l_ctx, col < l_ctx + BLOCK).astype(F32)
        hi_cols = (col >= l_ctx + 2 * BLOCK).astype(F32)
        edge_lo = jnp.where(i == 0, NEG_INF, 0.0)
        edge_hi = jnp.where(i == last, NEG_INF, 0.0)
    else:
        kc_ref, vc_ref, o_ref = rest
    kc = kc_ref[0]
    vc = vc_ref[0]
    lane_k = lax.broadcasted_iota(jnp.int32, (1, LANES), 1)
    even_k = lane_k < HEAD_DIM
    zero = jnp.zeros((), BF16)
    lane_o = lax.broadcasted_iota(jnp.int32, (BLOCK, LANES), 1)

    for j in range(nsub):
        r0 = j * BLOCK
        pairs = []
        for g in range(N_KV_HEADS):
            c0 = g * 2 * LANES
            lhs = jnp.concatenate([q_ref[0, r0:r0 + BLOCK, c0:c0 + LANES],
                                   q_ref[0, r0:r0 + BLOCK, c0 + LANES:c0 + 2 * LANES]], axis=0)
            keys = kc[:, g * LANES:(g + 1) * LANES]
            vals = vc[:, g * LANES:(g + 1) * LANES]
            if local:
                keys = jnp.concatenate([keys, kcat[r0:r0 + 3 * BLOCK, g * LANES:(g + 1) * LANES]], axis=0)
                vals = jnp.concatenate([vals, vcat[r0:r0 + 3 * BLOCK, g * LANES:(g + 1) * LANES]], axis=0)
                bj = bias
                if j == 0:
                    bj = bj + lo_cols * edge_lo
                if j == nsub - 1:
                    bj = bj + hi_cols * edge_hi
            halves = []
            for par in range(2):
                sel = even_k if par == 0 else jnp.logical_not(even_k)
                s = _dot_t(lhs, jnp.where(sel, keys, zero))
                if local:
                    s = s + bj
                sk = jnp.concatenate([jnp.full((BLOCK, 1), sink_ref[4 * g + par], F32),
                                      jnp.full((BLOCK, 1), sink_ref[4 * g + 2 + par], F32)], axis=0)
                m = jnp.maximum(jnp.max(s, axis=-1, keepdims=True), sk)
                p = jnp.exp(s - m)
                den = jnp.sum(p, axis=-1, keepdims=True) + jnp.exp(sk - m)
                halves.append(_dot(p.astype(BF16), vals) / den)
            for pr in range(2):
                pairs.append(jnp.where(lane_o < HEAD_DIM,
                                       halves[0][pr * BLOCK:(pr + 1) * BLOCK],
                                       halves[1][pr * BLOCK:(pr + 1) * BLOCK]))
        o_ref[0, r0:r0 + BLOCK, :] = jnp.concatenate(pairs, axis=1).astype(BF16)


def _attn_call(q, kd, vd, kcd, vcd, sink, bias, *, local):
    b, n, _ = q.shape
    l_ctx = kcd.shape[1]
    tq = min(ATTN_TILE, n)
    nsub = tq // BLOCK
    nblk = n // BLOCK
    kw = 2 * KV_WIDTH
    in_specs = [pl.BlockSpec(memory_space=pltpu.SMEM),
                pl.BlockSpec((1, tq, ATTN_WIDTH), lambda bi, i: (bi, i, 0))]
    args = [sink, q]
    if local:
        prev_map = lambda bi, i: (bi, jnp.maximum(i * nsub - 1, 0), 0)
        own_map = lambda bi, i: (bi, i, 0)
        next_map = lambda bi, i: (bi, jnp.minimum((i + 1) * nsub, nblk - 1), 0)
        for arr in (kd, vd):
            in_specs += [pl.BlockSpec((1, BLOCK, kw), prev_map),
                         pl.BlockSpec((1, tq, kw), own_map),
                         pl.BlockSpec((1, BLOCK, kw), next_map)]
            args += [arr, arr, arr]
    in_specs += [pl.BlockSpec((1, l_ctx, kw), lambda bi, i: (bi, 0, 0)),
                 pl.BlockSpec((1, l_ctx, kw), lambda bi, i: (bi, 0, 0))]
    args += [kcd, vcd]
    if local:
        in_specs.append(pl.BlockSpec((2 * BLOCK, l_ctx + 3 * BLOCK), lambda bi, i: (0, 0)))
        args.append(bias)
    return pl.pallas_call(
        functools.partial(_attn_kernel, nsub=nsub, local=local),
        grid=(b, n // tq),
        in_specs=in_specs,
        out_specs=pl.BlockSpec((1, tq, ATTN_WIDTH), lambda bi, i: (bi, i, 0)),
        out_shape=jax.ShapeDtypeStruct((b, n, ATTN_WIDTH), BF16),
        compiler_params=_cparams(("parallel", "arbitrary")),
        name="window_attention" if local else "context_attention",
    )(*args)


def _fft1_kernel(w_ref, x_ref, z_ref):
    z_ref[0] = _dot(w_ref[...].astype(BF16), x_ref[0]).astype(BF16)


def _fft2_kernel(t_ref, z_ref, cs_ref, o_ref, *, kb):
    n2 = z_ref.shape[3]
    cs = cs_ref[...].astype(BF16)
    for j in range(kb):
        zk = jnp.concatenate([z_ref[0, 0, j], z_ref[0, 1, j]], axis=0)
        y = _dot(t_ref[j].astype(BF16), zk)
        yc = jnp.concatenate([y[0:n2], y[n2:2 * n2]], axis=1).astype(BF16)
        o_ref[0, :, j, :] = _dot(yc, cs).astype(BF16)


def _fft_dense_kernel(cn_ref, sn_ref, cc_ref, sc_ref, x_ref, o_ref):
    x = x_ref[0]
    a = _dot(x, cc_ref[...].astype(BF16)).astype(BF16)
    bm = _dot(x, sc_ref[...].astype(BF16)).astype(BF16)
    o_ref[0] = (_dot(cn_ref[...].astype(BF16), a) - _dot(sn_ref[...].astype(BF16), bm)).astype(BF16)


def _channel_dft():
    c = np.arange(FNET_GROUP_DIM)
    ang = 2.0 * np.pi * np.outer(c, c) / FNET_GROUP_DIM
    eye = np.eye(FNET_GROUPS)
    cc = np.kron(eye, np.cos(ang)) / math.sqrt(FNET_GROUP_DIM)
    sc = np.kron(eye, np.sin(ang)) / math.sqrt(FNET_GROUP_DIM)
    return cc, sc


@functools.lru_cache(maxsize=None)
def _fft_tables(n):
    n2 = LANES
    n1 = n // n2
    t1 = np.arange(n1)
    a1 = 2.0 * np.pi * np.outer(t1, t1) / n1
    w1 = np.concatenate([np.cos(a1), -np.sin(a1)], axis=0) / math.sqrt(n)
    k = np.arange(n1)[:, None, None] + n1 * np.arange(n2)[None, :, None]
    a2 = 2.0 * np.pi * k * np.arange(n2)[None, None, :] / n
    mr, mi = np.cos(a2), -np.sin(a2)
    tb = np.concatenate([np.concatenate([mr, -mi], axis=2),
                         np.concatenate([mi, mr], axis=2)], axis=1)
    cc, sc = _channel_dft()
    cs = np.concatenate([cc, sc], axis=0)
    return (jnp.asarray(w1, F32), jnp.asarray(tb, F32), jnp.asarray(cs, F32))


@functools.lru_cache(maxsize=None)
def _fft_dense_tables(n):
    t = np.arange(n)
    a = 2.0 * np.pi * np.outer(t, t) / n
    cc, sc = _channel_dft()
    return (jnp.asarray(np.cos(a) / math.sqrt(n), F32), jnp.asarray(np.sin(a) / math.sqrt(n), F32),
            jnp.asarray(cc, F32), jnp.asarray(sc, F32))


def _fourier_call(f):
    b, n, c = f.shape
    if n <= 2 * LANES:
        cn, sn, cc, sc = _fft_dense_tables(n)
        full = lambda bi: (0, 0)
        return pl.pallas_call(
            _fft_dense_kernel,
            grid=(b,),
            in_specs=[pl.BlockSpec((n, n), full), pl.BlockSpec((n, n), full),
                      pl.BlockSpec((c, c), full), pl.BlockSpec((c, c), full),
                      pl.BlockSpec((1, n, c), lambda bi: (bi, 0, 0))],
            out_specs=pl.BlockSpec((1, n, c), lambda bi: (bi, 0, 0)),
            out_shape=jax.ShapeDtypeStruct((b, n, c), BF16),
            compiler_params=_cparams(("parallel",)),
            name="fourier_dense",
        )(cn, sn, cc, sc, f)
    n2 = LANES
    n1 = n // n2
    w1, tb, cs = _fft_tables(n)
    cols = n2 * c
    tc = 2048
    z = pl.pallas_call(
        _fft1_kernel,
        grid=(b, cols // tc),
        in_specs=[pl.BlockSpec((2 * n1, n1), lambda bi, j: (0, 0)),
                  pl.BlockSpec((1, n1, tc), lambda bi, j: (bi, 0, j))],
        out_specs=pl.BlockSpec((1, 2 * n1, tc), lambda bi, j: (bi, 0, j)),
        out_shape=jax.ShapeDtypeStruct((b, 2 * n1, cols), BF16),
        compiler_params=_cparams(("parallel", "arbitrary")),
        name="fourier_stage1",
    )(w1, f.reshape(b, n1, cols))
    kb = SUBLANES
    out = pl.pallas_call(
        functools.partial(_fft2_kernel, kb=kb),
        grid=(n1 // kb, b),
        in_specs=[pl.BlockSpec((kb, 2 * n2, 2 * n2), lambda j, bi: (j, 0, 0)),
                  pl.BlockSpec((1, 2, kb, n2, c), lambda j, bi: (bi, 0, j, 0, 0)),
                  pl.BlockSpec((2 * c, c), lambda j, bi: (0, 0))],
        out_specs=pl.BlockSpec((1, n2, kb, c), lambda j, bi: (bi, 0, j, 0)),
        out_shape=jax.ShapeDtypeStruct((b, n2, n1, c), BF16),
        compiler_params=_cparams(("arbitrary", "arbitrary")),
        name="fourier_stage2",
    )(tb, z.reshape(b, 2, n1, n2, c), cs)
    return out.reshape(b, n, c)


def _merge_kernel(a_ref, u_ref, vs_ref, fo_ref, gt_ref, x_ref, m_ref, g2_ref,
                  wsp_ref, bsp_ref, bd_ref, wba_ref, wbs_ref, wbf_ref, wo_ref, wr_ref,
                  xo_ref, h2_ref, aff_ref, s_scr):
    tm = a_ref.shape[1]
    lane = lax.broadcasted_iota(jnp.int32, (CHUNK, SGU_WIDTH), 1)
    for c in range(tm // CHUNK):
        r0 = c * CHUNK
        ug = jax.nn.gelu(u_ref[0, r0:r0 + CHUNK, :].astype(F32))
        vg = jax.nn.gelu(vs_ref[0, r0:r0 + CHUNK, :].astype(F32))
        sq = vg * vg
        sq_hi = sq.astype(BF16)
        sq_lo = (sq - sq_hi.astype(F32)).astype(BF16)
        msq = _dot(sq_hi, bd_ref[...]) + _dot(sq_lo, bd_ref[...])
        vn = (vg * lax.rsqrt(msq + EPS)).astype(BF16)
        zero = jnp.zeros((), BF16)
        stack = jnp.concatenate(
            [jnp.where(lax.shift_right_logical(lane, 6) == g, vn, zero) for g in range(SGU_GROUPS)], axis=0)
        z = _dot(wsp_ref[...], stack) + bsp_ref[...]
        s_scr[r0:r0 + CHUNK, :] = (ug * z).astype(BF16)

    gates = gt_ref[0]
    acc = jax.nn.sigmoid(gates[:, 0:D_MODEL].astype(F32)) * _dot(a_ref[0], wba_ref[...])
    acc = acc + jax.nn.sigmoid(gates[:, D_MODEL:2 * D_MODEL].astype(F32)) * _dot(s_scr[...], wbs_ref[...])
    acc = acc + jax.nn.sigmoid(gates[:, 2 * D_MODEL:].astype(F32)) * _dot(fo_ref[0], wbf_ref[...])
    o = _dot(acc.astype(BF16), wo_ref[...])
    xn = _load_tokens(x_ref) + m_ref[0, 2:3, :] * o
    _store_tokens(xo_ref, xn)
    ms = jnp.mean(xn * xn, axis=-1, keepdims=True)
    y = xn * lax.rsqrt(ms + EPS) * g2_ref[...]
    h2 = y * (1.0 + m_ref[0, 4:5, :]) + m_ref[0, 3:4, :]
    _store_tokens(h2_ref, h2)
    logits = _dot_t(wr_ref[...], h2.astype(BF16))
    mx = jnp.max(logits, axis=0, keepdims=True)
    ex = jnp.exp(logits - mx)
    aff_ref[0] = ex / jnp.sum(ex, axis=0, keepdims=True)


def _merge_call(a, u, vs, fo, gates, x, mod, mod_row, g2, wsp, bsp, bd, wba, wbs, wbf, wo, wr_t):
    b, n = x.shape[0], _num_tokens(x.shape)
    d = D_MODEL
    tm = min(TOKEN_TILE, n)
    if mod_row is None:
        mod_map = lambda bi, i: (bi, 0, 0)
    else:
        mod_map = lambda bi, i: (mod_row, 0, 0)
    tok = lambda wd: pl.BlockSpec((1, tm, wd), lambda bi, i: (bi, i, 0))
    full = lambda arr: pl.BlockSpec(arr.shape, lambda bi, i: (0,) * arr.ndim)
    tiled = (b, n * TILE_ROWS, LANES)
    return pl.pallas_call(
        _merge_kernel,
        grid=(b, n // tm),
        in_specs=[tok(ATTN_WIDTH), tok(SGU_WIDTH), tok(SGU_WIDTH), tok(FNET_WIDTH), tok(3 * D_MODEL),
                  _token_spec(tm, x.shape),
                  pl.BlockSpec((1, N_MOD, d), mod_map), full(g2),
                  full(wsp), full(bsp), full(bd), full(wba), full(wbs), full(wbf), full(wo), full(wr_t)],
        out_specs=[_token_spec(tm, tiled), _token_spec(tm, tiled),
                   pl.BlockSpec((1, N_EXPERTS, tm), lambda bi, i: (bi, 0, i))],
        out_shape=[jax.ShapeDtypeStruct(tiled, F32), jax.ShapeDtypeStruct(tiled, F32),
                   jax.ShapeDtypeStruct((b, N_EXPERTS, n), F32)],
        scratch_shapes=[pltpu.VMEM((tm, SGU_WIDTH), BF16)],
        compiler_params=_cparams(("parallel", "arbitrary")),
        name="merge_norm_router",
    )(a, u, vs, fo, gates, x, mod, g2, wsp, bsp, bd, wba, wbs, wbf, wo, wr_t)


def _select_kernel(aff_ref, tri_ref, lstrict_ref, idx_ref, *, cap, slots):
    rows = SEL_ROWS
    ones = jnp.ones((LANES, LANES), BF16)
    tri = tri_ref[...]
    lstrict = lstrict_ref[...]
    lane_r = lax.broadcasted_iota(jnp.int32, (rows, LANES), 1)
    row_r = lax.broadcasted_iota(jnp.int32, (rows, LANES), 0)
    eye = lane_r == row_r

    def total(mask_i32):
        return jnp.sum(jnp.sum(mask_i32, axis=0, keepdims=True), axis=1, keepdims=True)

    def cumsum(mask):
        mb = jnp.where(mask, 1.0, 0.0).astype(BF16)
        within = _dot(mb, tri)
        tot = jnp.broadcast_to(within[:, LANES - 1:LANES], (rows, LANES)).astype(BF16)
        return within + _dot(lstrict, tot)

    def split(v):
        hi = jnp.floor(v * (1.0 / 256.0))
        return hi.astype(BF16), (v - hi * 256.0).astype(BF16)

    aff_all = [aff_ref[0, e] for e in range(N_EXPERTS)]

    def as_f32(bits):
        return lax.bitcast_convert_type(bits, F32)

    def bit_step(t, prefixes):
        bit = lax.shift_left(jnp.int32(1), 30 - t)
        out = []
        for e in range(N_EXPERTS):
            cand = prefixes[e] | bit
            cnt = total((aff_all[e] >= as_f32(cand)).astype(jnp.int32))
            out.append(jnp.where(cnt >= cap, cand, prefixes[e]))
        return tuple(out)

    thr = lax.fori_loop(0, 31, bit_step, tuple(jnp.zeros((1, 1), jnp.int32) for _ in range(N_EXPERTS)))

    slot = lax.broadcasted_iota(jnp.int32, (slots, LANES), 0).astype(F32)
    lane_s = lax.broadcasted_iota(jnp.int32, (slots, LANES), 1).astype(F32)
    for e in range(N_EXPERTS):
        gt = aff_all[e] >= as_f32(thr[e] + 1)
        eq = jnp.logical_and(aff_all[e] >= as_f32(thr[e]), jnp.logical_not(gt))
        need = (cap - total(gt.astype(jnp.int32))).astype(F32)
        eq_rank = cumsum(eq) - jnp.where(eq, 1.0, 0.0)
        sel = jnp.logical_or(gt, jnp.logical_and(eq, eq_rank < need))
        gcum = cumsum(sel)
        ends = jnp.where(eye, jnp.broadcast_to(gcum[:, LANES - 1:LANES], (rows, LANES)), 0.0)
        e_hi, e_lo = split(ends)
        ones_s = jnp.ones((slots, rows), BF16)
        ends_row = _dot(ones_s, e_hi) * 256.0 + _dot(ones_s, e_lo)
        row_of = _dot(jnp.where(ends_row <= slot, 1.0, 0.0).astype(BF16), ones)
        onehot = jnp.where(lane_s == row_of, 1.0, 0.0).astype(BF16)
        g_hi, g_lo = split(gcum)
        grow = _dot(onehot, g_hi) * 256.0 + _dot(onehot, g_lo)
        lane_of = _dot(jnp.where(grow <= slot, 1.0, 0.0).astype(BF16), ones)
        token = (row_of * float(LANES) + lane_of).astype(jnp.int32)
        for blk in range(slots // LANES):
            tb = token[blk * LANES:(blk + 1) * LANES, :]
            idx_ref[0, e, blk:blk + 1, :] = jnp.sum(jnp.where(eye, tb, 0), axis=0, keepdims=True)


def _select_call(aff, cap):
    b, e, n = aff.shape
    full = SEL_ROWS * LANES
    if n < full:
        aff = jnp.pad(aff, ((0, 0), (0, 0), (0, full - n)), constant_values=-1.0)
    slots = max(LANES, cap)
    r = np.arange(LANES)
    tri = jnp.asarray(r[:, None] <= r[None, :], BF16)
    lstrict = jnp.asarray(r[None, :] < r[:, None], BF16)
    idx = pl.pallas_call(
        functools.partial(_select_kernel, cap=cap, slots=slots),
        grid=(b,),
        in_specs=[pl.BlockSpec((1, e, SEL_ROWS, LANES), lambda bi: (bi, 0, 0, 0)),
                  pl.BlockSpec((LANES, LANES), lambda bi: (0, 0)),
                  pl.BlockSpec((LANES, LANES), lambda bi: (0, 0))],
        out_specs=pl.BlockSpec((1, e, slots // LANES, LANES), lambda bi: (bi, 0, 0, 0)),
        out_shape=jax.ShapeDtypeStruct((b, e, slots // LANES, LANES), jnp.int32),
        compiler_params=_cparams(("parallel",)),
        name="expert_choice_select",
    )(aff.reshape(b, e, SEL_ROWS, LANES), tri, lstrict)
    return idx.reshape(b, e, slots)[:, :, :cap]


def _expert_kernel(idx_ref, idxn_ref, *refs, nsrc, nf, ne, r_lat, r_ctx, cap, rblk, gt_rows):
    h2_hbm = refs[0:nsrc]
    wg_ref, wu_ref, wd_ref, wr_ref, gt2_ref = refs[2 * nsrc:2 * nsrc + 5]
    x_hbm = refs[2 * nsrc + 5:3 * nsrc + 5]
    xe, acc, hst, xst, sem_h, sem_x = refs[3 * nsrc + 5:]
    e = pl.program_id(0)
    f = pl.program_id(1)
    cur = lax.rem(e, 2)
    nxt = 1 - cur
    nb = r_lat // rblk
    lat_step = r_lat // nf
    lat_it = lat_step // nb
    assert lat_step * nf == r_lat and lat_it * nb == lat_step and lat_step % DMA_UNROLL == 0
    assert r_ctx % DMA_UNROLL == 0 and rblk % DMA_UNROLL == 0 and cap % rblk == 0
    x_slots = xst.shape[0]

    def tile_dma(hbm, tok, buf, slot, pos, sem, to_vmem):
        h_tile = hbm.at[pl.ds(pl.multiple_of(tok * TILE_ROWS, TILE_ROWS), TILE_ROWS), :]
        v_tile = buf.at[slot, pl.ds(pl.multiple_of(pos * TILE_ROWS, TILE_ROWS), TILE_ROWS), :]
        if to_vmem:
            pltpu.make_async_copy(h_tile, v_tile, sem.at[slot]).start()
        else:
            pltpu.make_async_copy(v_tile, h_tile, sem.at[slot]).start()

    def issue_rows(iref, hbm, first, count, buf, slot, pos0, sem, to_vmem, inline):
        if inline:
            for s in range(count):
                tile_dma(hbm, iref[0, 0, first + s], buf, slot, pos0 + s, sem, to_vmem)
            return

        def body(t, carry):
            for s in range(DMA_UNROLL):
                r = t * DMA_UNROLL + s
                tile_dma(hbm, iref[0, 0, first + r], buf, slot, pos0 + r, sem, to_vmem)
            return carry
        lax.fori_loop(0, count // DMA_UNROLL, body, 0)

    def wait_tiles(hbm, buf, slot, ntiles, sem):
        pltpu.make_async_copy(hbm.at[pl.ds(0, ntiles * TILE_ROWS), :],
                              buf.at[slot, pl.ds(0, ntiles * TILE_ROWS), :], sem.at[slot]).wait()

    def staged(buf, slot, first_tile, ntiles, k):
        return (slot, pl.ds(first_tile * TILE_ROWS + k, ntiles, stride=TILE_ROWS), slice(None))

    def finish_chunk(c, dst):
        slot = c % 2
        wait_tiles(h2_hbm[0], hst, slot, lat_step, sem_h)
        if c == 0 and r_ctx:
            wait_tiles(h2_hbm[1], hst, 0, r_ctx, sem_h)
        for k in range(TILE_ROWS):
            cols = slice(k * LANES, (k + 1) * LANES)
            xe[dst, c * lat_step:(c + 1) * lat_step, cols] = hst[staged(hst, slot, 0, lat_step, k)].astype(BF16)
            if c == 0 and r_ctx:
                xe[dst, r_lat:r_lat + r_ctx, cols] = hst[staged(hst, 0, lat_step, r_ctx, k)].astype(BF16)

    for j in range(nf):
        @pl.when(f == j)
        def _land(j=j):
            if j == 0:
                @pl.when(e == 0)
                def _first_expert():
                    for c in range(nf):
                        issue_rows(idx_ref, h2_hbm[0], c * lat_step, lat_step, hst, c % 2, 0, sem_h, True, False)
                        if c == 0 and r_ctx:
                            issue_rows(idx_ref, h2_hbm[1], r_lat, r_ctx, hst, 0, lat_step, sem_h, True, False)
                        finish_chunk(c, cur)

                @pl.when(e > 0)
                def _last_chunk():
                    finish_chunk(nf - 1, cur)
                acc[...] = jnp.zeros_like(acc)
                if r_ctx:
                    issue_rows(idxn_ref, h2_hbm[1], r_lat, r_ctx, hst, 0, lat_step, sem_h, True, False)
            else:
                finish_chunk(j - 1, nxt)

    wg = wg_ref[0, 0].astype(BF16)
    wu = wu_ref[0, 0].astype(BF16)
    wd = wd_ref[0, 0].astype(BF16)
    hslot = lax.rem(f, 2)

    def ffn_rows(r0, nrows):
        xs = xe[cur, pl.ds(r0, nrows), :]
        a = _dot(xs, wg)
        u = _dot(xs, wu)
        hm = (a * jax.nn.sigmoid(a) * u).astype(BF16)
        acc[pl.ds(r0, nrows), :] += _dot(hm, wd)

    def prefetch_lat(rb):
        issue_rows(idxn_ref, h2_hbm[0], f * lat_step + rb * lat_it, lat_it, hst, hslot, rb * lat_it,
                   sem_h, True, True)

    @pl.when(f < nf - 1)
    def _plain_step():
        def body(rb, carry):
            prefetch_lat(rb)
            ffn_rows(pl.multiple_of(rb * rblk, rblk), rblk)
            return carry
        lax.fori_loop(0, nb, body, 0)
        if r_ctx:
            ffn_rows(r_lat, r_ctx)

    @pl.when(f == nf - 1)
    def _last_step():
        blocks = [(i * rblk, rblk, 0, gt_rows[0][(i * rblk) // cap]) for i in range(nb)]
        if r_ctx:
            blocks.append((r_lat, r_ctx, 1, gt_rows[1][0]))
        last = len(blocks) - 1

        def gather_x(i, inline):
            r0, nrows, src, _ = blocks[i]
            issue_rows(idx_ref, x_hbm[src], r0, nrows, xst, i % x_slots, 0, sem_x, True, inline)

        def wait_x(i):
            _, nrows, src, _ = blocks[i]
            wait_tiles(x_hbm[src], xst, i % x_slots, nrows, sem_x)

        def update_and_scatter(i, inline):
            r0, nrows, src, gt_row = blocks[i]
            slot = i % x_slots
            wait_x(i)
            lane = lax.broadcasted_iota(jnp.int32, (nrows, LANES), 1)
            logits = _dot(xe[cur, r0:r0 + nrows, :], wr_ref[...])
            logits = jnp.where(lane < N_EXPERTS, logits, NEG_INF)
            ex = jnp.exp(logits - jnp.max(logits, axis=-1, keepdims=True))
            gval = (jnp.sum(jnp.where(lane == e, ex, 0.0), axis=-1, keepdims=True)
                    / jnp.sum(ex, axis=-1, keepdims=True))
            for k in range(TILE_ROWS):
                cols = slice(k * LANES, (k + 1) * LANES)
                ix = staged(xst, slot, 0, nrows, k)
                xst[ix] = xst[ix] + gt2_ref[gt_row:gt_row + 1, cols] * (acc[r0:r0 + nrows, cols] * gval)
            issue_rows(idx_ref, x_hbm[src], r0, nrows, xst, slot, 0, sem_x, False, inline)

        gather_x(0, False)
        for i, (r0, nrows, _, _) in enumerate(blocks):
            if i >= 1:
                update_and_scatter(i - 1, True)
            if i < nb:
                prefetch_lat(i)
            if i >= 2:
                wait_x(i - 2)
            if i + 1 <= last:
                gather_x(i + 1, True)
            ffn_rows(r0, nrows)
        update_and_scatter(last, False)
        if last >= 1:
            wait_x(last - 1)
        wait_x(last)

        @pl.when(e == ne - 1)
        def _drain():
            wait_tiles(h2_hbm[0], hst, (nf - 1) % 2, lat_step, sem_h)


def _expert_choice(groups, layer, w_gate, w_up, w_down, wr_pad, gt2):
    d = D_MODEL
    ff = w_gate.shape[-1]
    tf = FF_TILE
    nf = ff // tf
    idx_parts, h2s, xs, counts, caps = [], [], [], [], []
    for h2, aff, x, _ in groups:
        b, n = x.shape[0], _num_tokens(x.shape)
        cap = CAPACITY_FACTOR * n // N_EXPERTS
        idx = _select_call(aff, cap)
        flat = idx + (jnp.arange(b, dtype=jnp.int32) * n)[:, None, None]
        idx_parts.append(jnp.transpose(flat, (1, 0, 2)).reshape(N_EXPERTS, b * cap))
        counts.append(b * cap)
        caps.append(cap)
        h2s.append(h2.reshape(b * n * TILE_ROWS, LANES))
        xs.append(x.reshape(b * n * TILE_ROWS, LANES))
    nsrc = len(groups)
    r_lat = counts[0]
    r_ctx = counts[1] if nsrc > 1 else 0
    rows = r_lat + r_ctx
    rblk = min(ROW_BLOCK, caps[0])
    idx_all = jnp.concatenate(idx_parts, axis=1).reshape(N_EXPERTS, 1, rows)
    stage_tiles = r_lat // nf + r_ctx
    any_spec = pl.BlockSpec(memory_space=pl.ANY)
    idx_spec = lambda shift: pl.BlockSpec(
        (1, 1, rows), lambda ei, fi: (jnp.minimum(ei + shift, N_EXPERTS - 1), 0, 0), memory_space=pltpu.SMEM)
    outs = pl.pallas_call(
        functools.partial(_expert_kernel, nsrc=nsrc, nf=nf, ne=N_EXPERTS, r_lat=r_lat, r_ctx=r_ctx,
                          cap=caps[0], rblk=rblk, gt_rows=tuple(tuple(g[3]) for g in groups)),
        grid=(N_EXPERTS, nf),
        in_specs=[idx_spec(0), idx_spec(1)]
        + [any_spec] * (2 * nsrc)
        + [pl.BlockSpec((1, 1, d, tf), lambda ei, fi: (layer, ei, 0, fi)),
           pl.BlockSpec((1, 1, d, tf), lambda ei, fi: (layer, ei, 0, fi)),
           pl.BlockSpec((1, 1, tf, d), lambda ei, fi: (layer, ei, fi, 0)),
           pl.BlockSpec((d, LANES), lambda ei, fi: (0, 0)),
           pl.BlockSpec(gt2.shape, lambda ei, fi: (0, 0))],
        out_specs=[any_spec] * nsrc,
        out_shape=[jax.ShapeDtypeStruct(xa.shape, F32) for xa in xs],
        scratch_shapes=[pltpu.VMEM((2, rows, d), BF16), pltpu.VMEM((rows, d), F32),
                        pltpu.VMEM((2, stage_tiles * TILE_ROWS, LANES), F32),
                        pltpu.VMEM((X_SLOTS, rblk * TILE_ROWS, LANES), F32),
                        pltpu.SemaphoreType.DMA((2,)), pltpu.SemaphoreType.DMA((X_SLOTS,))],
        input_output_aliases={2 + nsrc + s: s for s in range(nsrc)},
        compiler_params=_cparams(("arbitrary", "arbitrary")),
        name="expert_ffn",
    )(idx_all, idx_all, *h2s, *xs, w_gate, w_up, w_down, wr_pad, gt2)
    return [o.reshape(g[2].shape) for o, g in zip(outs, groups)]


def _final_norm_kernel(x_ref, g_ref, o_ref):
    x = _load_tokens(x_ref)
    ms = jnp.mean(x * x, axis=-1, keepdims=True)
    o_ref[0] = x * lax.rsqrt(ms + EPS) * g_ref[...]


def _final_norm_call(x, g):
    b, n = x.shape[0], _num_tokens(x.shape)
    d = D_MODEL
    tm = min(TOKEN_TILE, n)
    return pl.pallas_call(
        _final_norm_kernel,
        grid=(b, n // tm),
        in_specs=[_token_spec(tm, x.shape),
                  pl.BlockSpec((1, d), lambda bi, i: (0, 0))],
        out_specs=pl.BlockSpec((1, tm, d), lambda bi, i: (bi, i, 0)),
        out_shape=jax.ShapeDtypeStruct((b, n, d), F32),
        compiler_params=_cparams(("parallel", "arbitrary")),
        name="final_norm",
    )(x, g)


def _rope_tables(n):
    rows = n // GRID_W
    row = np.repeat(np.arange(rows, dtype=np.float32), GRID_W)
    col = np.tile(np.arange(GRID_W, dtype=np.float32), rows)
    pairs = HEAD_DIM // 4
    inv = (np.float32(ROPE_BASE) ** (-np.arange(pairs, dtype=np.float32) / pairs)).astype(np.float32)
    ang = np.concatenate([row[:, None] * inv, col[:, None] * inv], axis=-1).astype(np.float32)
    cos, sin = np.cos(ang), np.sin(ang)
    cos_t = np.concatenate([cos, cos, cos, cos], axis=-1)
    sin_t = np.concatenate([-sin, sin, -sin, sin], axis=-1)
    return jnp.asarray(cos_t, F32), jnp.asarray(sin_t, F32)


def _band_bias(l_ctx):
    qi = np.arange(BLOCK)[:, None]
    kj = np.arange(3 * BLOCK)[None, :]
    ok = np.abs(kj - BLOCK - qi) <= WINDOW
    bias = np.concatenate([np.zeros((BLOCK, l_ctx)), np.where(ok, 0.0, NEG_INF)], axis=1).astype(np.float32)
    return jnp.asarray(np.concatenate([bias, bias], axis=0))


def kernel(x, c, ctx, c_ctx, w_mod, b_mod, g_mix, g_ffn, w_in, attn_sink, w_spatial, b_spatial,
           w_branch_attn, w_branch_sgu, w_branch_fourier, w_out, w_router, w_gate, w_up, w_down, g_final):
    b, n, d = x.shape
    l_ctx = ctx.shape[1]
    depth = w_mod.shape[0]

    cvec = jnp.zeros((SUBLANES, d), F32).at[0:b].set(c).at[b].set(c_ctx)
    mod = _mod_call(cvec, w_mod, b_mod).reshape(depth, SUBLANES, N_MOD, d)
    ctx_row = b

    cos_t, sin_t = _rope_tables(n)
    bias = _band_bias(l_ctx)
    no_rope = jnp.zeros((l_ctx, LANES), F32)
    gsz = SGU_WIDTH // SGU_GROUPS
    bd = jnp.asarray(np.kron(np.eye(SGU_GROUPS), np.ones((gsz, gsz))) / gsz, BF16)

    for layer in range(depth):
        last = layer == depth - 1
        w_l = w_in[layer].astype(BF16)
        g1 = g_mix[layer].reshape(1, d)
        g2 = g_ffn[layer].reshape(1, d)
        wsp = jnp.transpose(w_spatial[layer], (1, 0, 2)).reshape(CHUNK, SGU_GROUPS * CHUNK).astype(BF16)
        bsp = jnp.repeat(jnp.transpose(b_spatial[layer]), gsz, axis=1)
        wba = w_branch_attn[layer].astype(BF16)
        wbs = w_branch_sgu[layer].astype(BF16)
        wbf = w_branch_fourier[layer].astype(BF16)
        wo = w_out[layer].astype(BF16)
        wr_t = jnp.transpose(w_router[layer]).astype(BF16)
        wr_pad = jnp.pad(w_router[layer], ((0, 0), (0, LANES - N_EXPERTS))).astype(BF16)
        sink = attn_sink[layer]
        mod_l = mod[layer]

        qc, kcd, vcd, uc, vsc, fc, gc = _proj_call(ctx, mod_l, ctx_row, g1, w_l, no_rope, no_rope, rope=False)
        if not last:
            a_c = _attn_call(qc, None, None, kcd, vcd, sink, None, local=False)
            f_c = _fourier_call(fc)
            ctx_mid, h2c, aff_c = _merge_call(a_c, uc, vsc, f_c, gc, ctx, mod_l, ctx_row, g2,
                                              wsp, bsp, bd, wba, wbs, wbf, wo, wr_t)

        q, kd, vd, u, vs, f, gates = _proj_call(x, mod_l, None, g1, w_l, cos_t, sin_t, rope=True)
        a = _attn_call(q, kd, vd, kcd, vcd, sink, bias, local=True)
        fo = _fourier_call(f)
        x_mid, h2, aff = _merge_call(a, u, vs, fo, gates, x, mod_l, None, g2,
                                     wsp, bsp, bd, wba, wbs, wbf, wo, wr_t)

        gt2 = mod_l[:, 5]
        groups = [(h2, aff, x_mid, list(range(b)))]
        if not last:
            groups.append((h2c, aff_c, ctx_mid, [ctx_row] * b))
        outs = _expert_choice(groups, layer, w_gate, w_up, w_down, wr_pad, gt2)
        x = outs[0]
        if not last:
            ctx = outs[1]
    return _final_norm_call(x, g_final.reshape(1, d))
```

```python
import functools
import math

import numpy as np
import jax
import jax.numpy as jnp
from jax import lax
from jax.experimental import pallas as pl
from jax.experimental.pallas import tpu as pltpu

F32 = jnp.float32
BF16 = jnp.bfloat16

D_MODEL = 1024
GRID_W = 64
HEAD_DIM = 64
N_HEADS = 8
N_KV_HEADS = 2
ATTN_WIDTH = N_HEADS * HEAD_DIM
KV_WIDTH = N_KV_HEADS * HEAD_DIM
WINDOW = 128
BLOCK = 128
ROPE_BASE = 10000.0
SGU_GROUPS = 4
SGU_WIDTH = 256
CHUNK = 128
FNET_GROUPS = 4
FNET_GROUP_DIM = 64
FNET_WIDTH = 256
N_EXPERTS = 16
CAPACITY_FACTOR = 2
EXPERT_FF = 2048
N_MOD = 6
EPS = 1e-6
NEG_INF = -1e30
OFF_KV = ATTN_WIDTH
OFF_U = ATTN_WIDTH + 2 * KV_WIDTH
OFF_G = OFF_U + 2 * SGU_WIDTH + FNET_WIDTH
IN_WIDTH = OFF_G + 3 * D_MODEL

LANES = 128
SUBLANES = 8
VMEM_LIMIT_BYTES = 56 * 1024 * 1024

TOKEN_TILE = 512
ATTN_TILE = 256
SEL_ROWS = 128
FF_TILE = 256
ROW_BLOCK = 512
DMA_UNROLL = 8
X_SLOTS = 4


def _cparams(sem):
    return pltpu.CompilerParams(dimension_semantics=sem, vmem_limit_bytes=VMEM_LIMIT_BYTES)


def _dot(a, b):
    return jnp.dot(a, b, preferred_element_type=F32)


def _dot_t(a, b):
    return lax.dot_general(a, b, (((1,), (1,)), ((), ())), preferred_element_type=F32)


def _mod_kernel(c_ref, w_ref, b_ref, o_ref):
    c = c_ref[...]
    s = (c * jax.nn.sigmoid(c)).astype(BF16)
    o_ref[0] = _dot(s, w_ref[0].astype(BF16)) + b_ref[0]


def _mod_call(cvec, w_mod, b_mod):
    depth, d, width = w_mod.shape
    tn = 1536
    return pl.pallas_call(
        _mod_kernel,
        grid=(depth, width // tn),
        in_specs=[
            pl.BlockSpec((SUBLANES, d), lambda l, j: (0, 0)),
            pl.BlockSpec((1, d, tn), lambda l, j: (l, 0, j)),
            pl.BlockSpec((1, 1, tn), lambda l, j: (l, 0, j)),
        ],
        out_specs=pl.BlockSpec((1, SUBLANES, tn), lambda l, j: (l, 0, j)),
        out_shape=jax.ShapeDtypeStruct((depth, SUBLANES, width), F32),
        compiler_params=_cparams(("arbitrary", "arbitrary")),
        name="adaln_mod",
    )(cvec, w_mod, b_mod.reshape(depth, 1, width))


def _swap_halves(t):
    lane = lax.broadcasted_iota(jnp.int32, t.shape, 1)
    first = (lane % HEAD_DIM) < (HEAD_DIM // 2)
    return jnp.where(first, pltpu.roll(t, LANES - HEAD_DIM // 2, 1), pltpu.roll(t, HEAD_DIM // 2, 1))


def _dup_heads(t):
    lane = lax.broadcasted_iota(jnp.int32, t.shape, 1)
    lo = lane < HEAD_DIM
    r = pltpu.roll(t, HEAD_DIM, 1)
    return jnp.concatenate([jnp.where(lo, t, r), jnp.where(lo, r, t)], axis=1)


TILE_ROWS = D_MODEL // LANES


def _is_row_layout(shape):
    return shape[-1] == D_MODEL


def _num_tokens(shape):
    return shape[1] if _is_row_layout(shape) else shape[1] // TILE_ROWS


def _load_tokens(ref):
    if _is_row_layout(ref.shape):
        return ref[0]
    tm = ref.shape[1] // TILE_ROWS
    return jnp.concatenate([ref[0, pl.ds(k, tm, stride=TILE_ROWS), :] for k in range(TILE_ROWS)], axis=1)


def _store_tokens(ref, val):
    if _is_row_layout(ref.shape):
        ref[0] = val
    else:
        tm = ref.shape[1] // TILE_ROWS
        for k in range(TILE_ROWS):
            ref[0, pl.ds(k, tm, stride=TILE_ROWS), :] = val[:, k * LANES:(k + 1) * LANES]


def _token_spec(tm, shape):
    rows = tm if _is_row_layout(shape) else tm * TILE_ROWS
    return pl.BlockSpec((1, rows, shape[2]), lambda bi, i: (bi, i, 0))


def _proj_kernel(x_ref, m_ref, g_ref, w_ref, cos_ref, sin_ref,
                 q_ref, k_ref, v_ref, u_ref, vs_ref, f_ref, gt_ref, *, rope):
    x = _load_tokens(x_ref)
    ms = jnp.mean(x * x, axis=-1, keepdims=True)
    y = x * lax.rsqrt(ms + EPS) * g_ref[...]
    h = y * (1.0 + m_ref[0, 1:2, :]) + m_ref[0, 0:1, :]
    hb = h.astype(BF16)

    if rope:
        cos = cos_ref[...]
        sin = sin_ref[...]

    def rot(t):
        if not rope:
            return t
        return t * cos + _swap_halves(t) * sin

    q = _dot(hb, w_ref[:, 0:OFF_KV])
    for j in range(ATTN_WIDTH // LANES):
        qj = rot(q[:, j * LANES:(j + 1) * LANES]) * (HEAD_DIM ** -0.5)
        q_ref[0, :, j * LANES:(j + 1) * LANES] = qj.astype(BF16)
    kv = _dot(hb, w_ref[:, OFF_KV:OFF_U])
    k_ref[0] = _dup_heads(rot(kv[:, 0:KV_WIDTH])).astype(BF16)
    v_ref[0] = _dup_heads(kv[:, KV_WIDTH:2 * KV_WIDTH]).astype(BF16)
    uvf = _dot(hb, w_ref[:, OFF_U:OFF_G])
    u_ref[0] = uvf[:, 0:SGU_WIDTH].astype(BF16)
    vs_ref[0] = uvf[:, SGU_WIDTH:2 * SGU_WIDTH].astype(BF16)
    f_ref[0] = uvf[:, 2 * SGU_WIDTH:].astype(BF16)
    for j in range(3):
        gj = _dot(hb, w_ref[:, OFF_G + j * D_MODEL:OFF_G + (j + 1) * D_MODEL])
        gt_ref[0, :, j * D_MODEL:(j + 1) * D_MODEL] = gj.astype(BF16)


def _proj_call(x, mod, mod_row, g, w, cos_t, sin_t, *, rope):
    b, n = x.shape[0], _num_tokens(x.shape)
    d = D_MODEL
    tm = min(TOKEN_TILE, n)
    if mod_row is None:
        mod_map = lambda bi, i: (bi, 0, 0)
    else:
        mod_map = lambda bi, i: (mod_row, 0, 0)
    widths = [ATTN_WIDTH, 2 * KV_WIDTH, 2 * KV_WIDTH, SGU_WIDTH, SGU_WIDTH, FNET_WIDTH, 3 * D_MODEL]
    return pl.pallas_call(
        functools.partial(_proj_kernel, rope=rope),
        grid=(b, n // tm),
        in_specs=[
            _token_spec(tm, x.shape),
            pl.BlockSpec((1, N_MOD, d), mod_map),
            pl.BlockSpec((1, d), lambda bi, i: (0, 0)),
            pl.BlockSpec((d, IN_WIDTH), lambda bi, i: (0, 0)),
            pl.BlockSpec((tm, LANES), lambda bi, i: (i, 0)),
            pl.BlockSpec((tm, LANES), lambda bi, i: (i, 0)),
        ],
        out_specs=[pl.BlockSpec((1, tm, wd), lambda bi, i: (bi, i, 0)) for wd in widths],
        out_shape=[jax.ShapeDtypeStruct((b, n, wd), BF16) for wd in widths],
        compiler_params=_cparams(("parallel", "arbitrary")),
        name="norm_in_proj",
    )(x, mod, g, w, cos_t, sin_t)


def _attn_kernel(sink_ref, q_ref, *rest, nsub, local):
    if local:
        kp_ref, ko_ref, kn_ref, vp_ref, vo_ref, vn_ref, kc_ref, vc_ref, bias_ref, o_ref = rest
        l_ctx = kc_ref.shape[1]
        i = pl.program_id(1)
        last = pl.num_programs(1) - 1
        kcat = jnp.concatenate([kp_ref[0], ko_ref[0], kn_ref[0]], axis=0)
        vcat = jnp.concatenate([vp_ref[0], vo_ref[0], vn_ref[0]], axis=0)
        bias = bias_ref[...]
        col = lax.broadcasted_iota(jnp.int32, (1, l_ctx + 3 * BLOCK), 1)
        lo_cols = jnp.logical_and(col >= l_ctx, col < l_ctx + BLOCK).astype(F32)
        hi_cols = (col >= l_ctx + 2 * BLOCK).astype(F32)
        edge_lo = jnp.where(i == 0, NEG_INF, 0.0)
        edge_hi = jnp.where(i == last, NEG_INF, 0.0)
    else:
        kc_ref, vc_ref, o_ref = rest
    kc = kc_ref[0]
    vc = vc_ref[0]
    lane_k = lax.broadcasted_iota(jnp.int32, (1, LANES), 1)
    even_k = lane_k < HEAD_DIM
    zero = jnp.zeros((), BF16)
    lane_o = lax.broadcasted_iota(jnp.int32, (BLOCK, LANES), 1)

    for j in range(nsub):
        r0 = j * BLOCK
        pairs = []
        for g in range(N_KV_HEADS):
            c0 = g * 2 * LANES
            lhs = jnp.concatenate([q_ref[0, r0:r0 + BLOCK, c0:c0 + LANES],
                                   q_ref[0, r0:r0 + BLOCK, c0 + LANES:c0 + 2 * LANES]], axis=0)
            keys = kc[:, g * LANES:(g + 1) * LANES]
            vals = vc[:, g * LANES:(g + 1) * LANES]
            if local:
                keys = jnp.concatenate([keys, kcat[r0:r0 + 3 * BLOCK, g * LANES:(g + 1) * LANES]], axis=0)
                vals = jnp.concatenate([vals, vcat[r0:r0 + 3 * BLOCK, g * LANES:(g + 1) * LANES]], axis=0)
                bj = bias
                if j == 0:
                    bj = bj + lo_cols * edge_lo
                if j == nsub - 1:
                    bj = bj + hi_cols * edge_hi
            halves = []
            for par in range(2):
                sel = even_k if par == 0 else jnp.logical_not(even_k)
                s = _dot_t(lhs, jnp.where(sel, keys, zero))
                if local:
                    s = s + bj
                sk = jnp.concatenate([jnp.full((BLOCK, 1), sink_ref[4 * g + par], F32),
                                      jnp.full((BLOCK, 1), sink_ref[4 * g + 2 + par], F32)], axis=0)
                m = jnp.maximum(jnp.max(s, axis=-1, keepdims=True), sk)
                p = jnp.exp(s - m)
                den = jnp.sum(p, axis=-1, keepdims=True) + jnp.exp(sk - m)
                halves.append(_dot(p.astype(BF16), vals) / den)
            for pr in range(2):
                pairs.append(jnp.where(lane_o < HEAD_DIM,
                                       halves[0][pr * BLOCK:(pr + 1) * BLOCK],
                                       halves[1][pr * BLOCK:(pr + 1) * BLOCK]))
        o_ref[0, r0:r0 + BLOCK, :] = jnp.concatenate(pairs, axis=1).astype(BF16)


def _attn_call(q, kd, vd, kcd, vcd, sink, bias, *, local):
    b, n, _ = q.shape
    l_ctx = kcd.shape[1]
    tq = min(ATTN_TILE, n)
    nsub = tq // BLOCK
    nblk = n // BLOCK
    kw = 2 * KV_WIDTH
    in_specs = [pl.BlockSpec(memory_space=pltpu.SMEM),
                pl.BlockSpec((1, tq, ATTN_WIDTH), lambda bi, i: (bi, i, 0))]
    args = [sink, q]
    if local:
        prev_map = lambda bi, i: (bi, jnp.maximum(i * nsub - 1, 0), 0)
        own_map = lambda bi, i: (bi, i, 0)
        next_map = lambda bi, i: (bi, jnp.minimum((i + 1) * nsub, nblk - 1), 0)
        for arr in (kd, vd):
            in_specs += [pl.BlockSpec((1, BLOCK, kw), prev_map),
                         pl.BlockSpec((1, tq, kw), own_map),
                         pl.BlockSpec((1, BLOCK, kw), next_map)]
            args += [arr, arr, arr]
    in_specs += [pl.BlockSpec((1, l_ctx, kw), lambda bi, i: (bi, 0, 0)),
                 pl.BlockSpec((1, l_ctx, kw), lambda bi, i: (bi, 0, 0))]
    args += [kcd, vcd]
    if local:
        in_specs.append(pl.BlockSpec((2 * BLOCK, l_ctx + 3 * BLOCK), lambda bi, i: (0, 0)))
        args.append(bias)
    return pl.pallas_call(
        functools.partial(_attn_kernel, nsub=nsub, local=local),
        grid=(b, n // tq),
        in_specs=in_specs,
        out_specs=pl.BlockSpec((1, tq, ATTN_WIDTH), lambda bi, i: (bi, i, 0)),
        out_shape=jax.ShapeDtypeStruct((b, n, ATTN_WIDTH), BF16),
        compiler_params=_cparams(("parallel", "arbitrary")),
        name="window_attention" if local else "context_attention",
    )(*args)


def _fft1_kernel(w_ref, x_ref, z_ref):
    z_ref[0] = _dot(w_ref[...].astype(BF16), x_ref[0]).astype(BF16)


def _fft2_kernel(t_ref, z_ref, cs_ref, o_ref, *, kb):
    n2 = z_ref.shape[3]
    cs = cs_ref[...].astype(BF16)
    for j in range(kb):
        zk = jnp.concatenate([z_ref[0, 0, j], z_ref[0, 1, j]], axis=0)
        y = _dot(t_ref[j].astype(BF16), zk)
        yc = jnp.concatenate([y[0:n2], y[n2:2 * n2]], axis=1).astype(BF16)
        o_ref[0, :, j, :] = _dot(yc, cs).astype(BF16)


def _fft_dense_kernel(cn_ref, sn_ref, cc_ref, sc_ref, x_ref, o_ref):
    x = x_ref[0]
    a = _dot(x, cc_ref[...].astype(BF16)).astype(BF16)
    bm = _dot(x, sc_ref[...].astype(BF16)).astype(BF16)
    o_ref[0] = (_dot(cn_ref[...].astype(BF16), a) - _dot(sn_ref[...].astype(BF16), bm)).astype(BF16)


def _channel_dft():
    c = np.arange(FNET_GROUP_DIM)
    ang = 2.0 * np.pi * np.outer(c, c) / FNET_GROUP_DIM
    eye = np.eye(FNET_GROUPS)
    cc = np.kron(eye, np.cos(ang)) / math.sqrt(FNET_GROUP_DIM)
    sc = np.kron(eye, np.sin(ang)) / math.sqrt(FNET_GROUP_DIM)
    return cc, sc


@functools.lru_cache(maxsize=None)
def _fft_tables(n):
    n2 = LANES
    n1 = n // n2
    t1 = np.arange(n1)
    a1 = 2.0 * np.pi * np.outer(t1, t1) / n1
    w1 = np.concatenate([np.cos(a1), -np.sin(a1)], axis=0) / math.sqrt(n)
    k = np.arange(n1)[:, None, None] + n1 * np.arange(n2)[None, :, None]
    a2 = 2.0 * np.pi * k * np.arange(n2)[None, None, :] / n
    mr, mi = np.cos(a2), -np.sin(a2)
    tb = np.concatenate([np.concatenate([mr, -mi], axis=2),
                         np.concatenate([mi, mr], axis=2)], axis=1)
    cc, sc = _channel_dft()
    cs = np.concatenate([cc, sc], axis=0)
    return (jnp.asarray(w1, F32), jnp.asarray(tb, F32), jnp.asarray(cs, F32))


@functools.lru_cache(maxsize=None)
def _fft_dense_tables(n):
    t = np.arange(n)
    a = 2.0 * np.pi * np.outer(t, t) / n
    cc, sc = _channel_dft()
    return (jnp.asarray(np.cos(a) / math.sqrt(n), F32), jnp.asarray(np.sin(a) / math.sqrt(n), F32),
            jnp.asarray(cc, F32), jnp.asarray(sc, F32))


def _fourier_call(f):
    b, n, c = f.shape
    if n <= 2 * LANES:
        cn, sn, cc, sc = _fft_dense_tables(n)
        full = lambda bi: (0, 0)
        return pl.pallas_call(
            _fft_dense_kernel,
            grid=(b,),
            in_specs=[pl.BlockSpec((n, n), full), pl.BlockSpec((n, n), full),
                      pl.BlockSpec((c, c), full), pl.BlockSpec((c, c), full),
                      pl.BlockSpec((1, n, c), lambda bi: (bi, 0, 0))],
            out_specs=pl.BlockSpec((1, n, c), lambda bi: (bi, 0, 0)),
            out_shape=jax.ShapeDtypeStruct((b, n, c), BF16),
            compiler_params=_cparams(("parallel",)),
            name="fourier_dense",
        )(cn, sn, cc, sc, f)
    n2 = LANES
    n1 = n // n2
    w1, tb, cs = _fft_tables(n)
    cols = n2 * c
    tc = 2048
    z = pl.pallas_call(
        _fft1_kernel,
        grid=(b, cols // tc),
        in_specs=[pl.BlockSpec((2 * n1, n1), lambda bi, j: (0, 0)),
                  pl.BlockSpec((1, n1, tc), lambda bi, j: (bi, 0, j))],
        out_specs=pl.BlockSpec((1, 2 * n1, tc), lambda bi, j: (bi, 0, j)),
        out_shape=jax.ShapeDtypeStruct((b, 2 * n1, cols), BF16),
        compiler_params=_cparams(("parallel", "arbitrary")),
        name="fourier_stage1",
    )(w1, f.reshape(b, n1, cols))
    kb = SUBLANES
    out = pl.pallas_call(
        functools.partial(_fft2_kernel, kb=kb),
        grid=(n1 // kb, b),
        in_specs=[pl.BlockSpec((kb, 2 * n2, 2 * n2), lambda j, bi: (j, 0, 0)),
                  pl.BlockSpec((1, 2, kb, n2, c), lambda j, bi: (bi, 0, j, 0, 0)),
                  pl.BlockSpec((2 * c, c), lambda j, bi: (0, 0))],
        out_specs=pl.BlockSpec((1, n2, kb, c), lambda j, bi: (bi, 0, j, 0)),
        out_shape=jax.ShapeDtypeStruct((b, n2, n1, c), BF16),
        compiler_params=_cparams(("arbitrary", "arbitrary")),
        name="fourier_stage2",
    )(tb, z.reshape(b, 2, n1, n2, c), cs)
    return out.reshape(b, n, c)


def _merge_kernel(a_ref, u_ref, vs_ref, fo_ref, gt_ref, x_ref, m_ref, g2_ref,
                  wsp_ref, bsp_ref, bd_ref, wba_ref, wbs_ref, wbf_ref, wo_ref, wr_ref,
                  xo_ref, h2_ref, aff_ref, s_scr):
    tm = a_ref.shape[1]
    lane = lax.broadcasted_iota(jnp.int32, (CHUNK, SGU_WIDTH), 1)
    for c in range(tm // CHUNK):
        r0 = c * CHUNK
        ug = jax.nn.gelu(u_ref[0, r0:r0 + CHUNK, :].astype(F32))
        vg = jax.nn.gelu(vs_ref[0, r0:r0 + CHUNK, :].astype(F32))
        sq = vg * vg
        sq_hi = sq.astype(BF16)
        sq_lo = (sq - sq_hi.astype(F32)).astype(BF16)
        msq = _dot(sq_hi, bd_ref[...]) + _dot(sq_lo, bd_ref[...])
        vn = (vg * lax.rsqrt(msq + EPS)).astype(BF16)
        zero = jnp.zeros((), BF16)
        stack = jnp.concatenate(
            [jnp.where(lax.shift_right_logical(lane, 6) == g, vn, zero) for g in range(SGU_GROUPS)], axis=0)
        z = _dot(wsp_ref[...], stack) + bsp_ref[...]
        s_scr[r0:r0 + CHUNK, :] = (ug * z).astype(BF16)

    gates = gt_ref[0]
    acc = jax.nn.sigmoid(gates[:, 0:D_MODEL].astype(F32)) * _dot(a_ref[0], wba_ref[...])
    acc = acc + jax.nn.sigmoid(gates[:, D_MODEL:2 * D_MODEL].astype(F32)) * _dot(s_scr[...], wbs_ref[...])
    acc = acc + jax.nn.sigmoid(gates[:, 2 * D_MODEL:].astype(F32)) * _dot(fo_ref[0], wbf_ref[...])
    o = _dot(acc.astype(BF16), wo_ref[...])
    xn = _load_tokens(x_ref) + m_ref[0, 2:3, :] * o
    _store_tokens(xo_ref, xn)
    ms = jnp.mean(xn * xn, axis=-1, keepdims=True)
    y = xn * lax.rsqrt(ms + EPS) * g2_ref[...]
    h2 = y * (1.0 + m_ref[0, 4:5, :]) + m_ref[0, 3:4, :]
    _store_tokens(h2_ref, h2)
    logits = _dot_t(wr_ref[...], h2.astype(BF16))
    mx = jnp.max(logits, axis=0, keepdims=True)
    ex = jnp.exp(logits - mx)
    aff_ref[0] = ex / jnp.sum(ex, axis=0, keepdims=True)


def _merge_call(a, u, vs, fo, gates, x, mod, mod_row, g2, wsp, bsp, bd, wba, wbs, wbf, wo, wr_t):
    b, n = x.shape[0], _num_tokens(x.shape)
    d = D_MODEL
    tm = min(TOKEN_TILE, n)
    if mod_row is None:
        mod_map = lambda bi, i: (bi, 0, 0)
    else:
        mod_map = lambda bi, i: (mod_row, 0, 0)
    tok = lambda wd: pl.BlockSpec((1, tm, wd), lambda bi, i: (bi, i, 0))
    full = lambda arr: pl.BlockSpec(arr.shape, lambda bi, i: (0,) * arr.ndim)
    tiled = (b, n * TILE_ROWS, LANES)
    return pl.pallas_call(
        _merge_kernel,
        grid=(b, n // tm),
        in_specs=[tok(ATTN_WIDTH), tok(SGU_WIDTH), tok(SGU_WIDTH), tok(FNET_WIDTH), tok(3 * D_MODEL),
                  _token_spec(tm, x.shape),
                  pl.BlockSpec((1, N_MOD, d), mod_map), full(g2),
                  full(wsp), full(bsp), full(bd), full(wba), full(wbs), full(wbf), full(wo), full(wr_t)],
        out_specs=[_token_spec(tm, tiled), _token_spec(tm, tiled),
                   pl.BlockSpec((1, N_EXPERTS, tm), lambda bi, i: (bi, 0, i))],
        out_shape=[jax.ShapeDtypeStruct(tiled, F32), jax.ShapeDtypeStruct(tiled, F32),
                   jax.ShapeDtypeStruct((b, N_EXPERTS, n), F32)],
        scratch_shapes=[pltpu.VMEM((tm, SGU_WIDTH), BF16)],
        compiler_params=_cparams(("parallel", "arbitrary")),
        name="merge_norm_router",
    )(a, u, vs, fo, gates, x, mod, g2, wsp, bsp, bd, wba, wbs, wbf, wo, wr_t)


def _select_kernel(aff_ref, tri_ref, lstrict_ref, idx_ref, *, cap, slots):
    rows = SEL_ROWS
    ones = jnp.ones((LANES, LANES), BF16)
    tri = tri_ref[...]
    lstrict = lstrict_ref[...]
    lane_r = lax.broadcasted_iota(jnp.int32, (rows, LANES), 1)
    row_r = lax.broadcasted_iota(jnp.int32, (rows, LANES), 0)
    eye = lane_r == row_r

    def total(mask_i32):
        return jnp.sum(jnp.sum(mask_i32, axis=0, keepdims=True), axis=1, keepdims=True)

    def cumsum(mask):
        mb = jnp.where(mask, 1.0, 0.0).astype(BF16)
        within = _dot(mb, tri)
        tot = jnp.broadcast_to(within[:, LANES - 1:LANES], (rows, LANES)).astype(BF16)
        return within + _dot(lstrict, tot)

    def split(v):
        hi = jnp.floor(v * (1.0 / 256.0))
        return hi.astype(BF16), (v - hi * 256.0).astype(BF16)

    aff_all = [aff_ref[0, e] for e in range(N_EXPERTS)]

    def as_f32(bits):
        return lax.bitcast_convert_type(bits, F32)

    def bit_step(t, prefixes):
        bit = lax.shift_left(jnp.int32(1), 30 - t)
        out = []
        for e in range(N_EXPERTS):
            cand = prefixes[e] | bit
            cnt = total((aff_all[e] >= as_f32(cand)).astype(jnp.int32))
            out.append(jnp.where(cnt >= cap, cand, prefixes[e]))
        return tuple(out)

    thr = lax.fori_loop(0, 31, bit_step, tuple(jnp.zeros((1, 1), jnp.int32) for _ in range(N_EXPERTS)))

    slot = lax.broadcasted_iota(jnp.int32, (slots, LANES), 0).astype(F32)
    lane_s = lax.broadcasted_iota(jnp.int32, (slots, LANES), 1).astype(F32)
    for e in range(N_EXPERTS):
        gt = aff_all[e] >= as_f32(thr[e] + 1)
        eq = jnp.logical_and(aff_all[e] >= as_f32(thr[e]), jnp.logical_not(gt))
        need = (cap - total(gt.astype(jnp.int32))).astype(F32)
        eq_rank = cumsum(eq) - jnp.where(eq, 1.0, 0.0)
        sel = jnp.logical_or(gt, jnp.logical_and(eq, eq_rank < need))
        gcum = cumsum(sel)
        ends = jnp.where(eye, jnp.broadcast_to(gcum[:, LANES - 1:LANES], (rows, LANES)), 0.0)
        e_hi, e_lo = split(ends)
        ones_s = jnp.ones((slots, rows), BF16)
        ends_row = _dot(ones_s, e_hi) * 256.0 + _dot(ones_s, e_lo)
        row_of = _dot(jnp.where(ends_row <= slot, 1.0, 0.0).astype(BF16), ones)
        onehot = jnp.where(lane_s == row_of, 1.0, 0.0).astype(BF16)
        g_hi, g_lo = split(gcum)
        grow = _dot(onehot, g_hi) * 256.0 + _dot(onehot, g_lo)
        lane_of = _dot(jnp.where(grow <= slot, 1.0, 0.0).astype(BF16), ones)
        token = (row_of * float(LANES) + lane_of).astype(jnp.int32)
        for blk in range(slots // LANES):
            tb = token[blk * LANES:(blk + 1) * LANES, :]
            idx_ref[0, e, blk:blk + 1, :] = jnp.sum(jnp.where(eye, tb, 0), axis=0, keepdims=True)


def _select_call(aff, cap):
    b, e, n = aff.shape
    full = SEL_ROWS * LANES
    if n < full:
        aff = jnp.pad(aff, ((0, 0), (0, 0), (0, full - n)), constant_values=-1.0)
    slots = max(LANES, cap)
    r = np.arange(LANES)
    tri = jnp.asarray(r[:, None] <= r[None, :], BF16)
    lstrict = jnp.asarray(r[None, :] < r[:, None], BF16)
    idx = pl.pallas_call(
        functools.partial(_select_kernel, cap=cap, slots=slots),
        grid=(b,),
        in_specs=[pl.BlockSpec((1, e, SEL_ROWS, LANES), lambda bi: (bi, 0, 0, 0)),
                  pl.BlockSpec((LANES, LANES), lambda bi: (0, 0)),
                  pl.BlockSpec((LANES, LANES), lambda bi: (0, 0))],
        out_specs=pl.BlockSpec((1, e, slots // LANES, LANES), lambda bi: (bi, 0, 0, 0)),
        out_shape=jax.ShapeDtypeStruct((b, e, slots // LANES, LANES), jnp.int32),
        compiler_params=_cparams(("parallel",)),
        name="expert_choice_select",
    )(aff.reshape(b, e, SEL_ROWS, LANES), tri, lstrict)
    return idx.reshape(b, e, slots)[:, :, :cap]


def _expert_kernel(idx_ref, idxn_ref, *refs, nsrc, nf, ne, r_lat, r_ctx, cap, rblk, gt_rows):
    h2_hbm = refs[0:nsrc]
    wg_ref, wu_ref, wd_ref, wr_ref, gt2_ref = refs[2 * nsrc:2 * nsrc + 5]
    x_hbm = refs[2 * nsrc + 5:3 * nsrc + 5]
    xe, acc, hst, xst, sem_h, sem_x = refs[3 * nsrc + 5:]
    e = pl.program_id(0)
    f = pl.program_id(1)
    cur = lax.rem(e, 2)
    nxt = 1 - cur
    nb = r_lat // rblk
    lat_step = r_lat // nf
    pre_blocks = max(1, nb // 2)
    lat_it = lat_step // pre_blocks
    assert lat_step * nf == r_lat and lat_it * pre_blocks == lat_step and lat_step % DMA_UNROLL == 0
    assert r_ctx % DMA_UNROLL == 0 and rblk % DMA_UNROLL == 0 and cap % rblk == 0
    x_slots = xst.shape[0]
    ahead = x_slots - 2

    def tile_dma(hbm, tok, buf, slot, pos, sem, to_vmem):
        h_tile = hbm.at[pl.ds(pl.multiple_of(tok * TILE_ROWS, TILE_ROWS), TILE_ROWS), :]
        v_tile = buf.at[slot, pl.ds(pl.multiple_of(pos * TILE_ROWS, TILE_ROWS), TILE_ROWS), :]
        if to_vmem:
            pltpu.make_async_copy(h_tile, v_tile, sem.at[slot]).start()
        else:
            pltpu.make_async_copy(v_tile, h_tile, sem.at[slot]).start()

    def issue_rows(iref, hbm, first, count, buf, slot, pos0, sem, to_vmem, inline):
        if inline:
            for s in range(count):
                tile_dma(hbm, iref[0, 0, first + s], buf, slot, pos0 + s, sem, to_vmem)
            return

        def body(t, carry):
            for s in range(DMA_UNROLL):
                r = t * DMA_UNROLL + s
                tile_dma(hbm, iref[0, 0, first + r], buf, slot, pos0 + r, sem, to_vmem)
            return carry
        lax.fori_loop(0, count // DMA_UNROLL, body, 0)

    def wait_tiles(hbm, buf, slot, ntiles, sem):
        pltpu.make_async_copy(hbm.at[pl.ds(0, ntiles * TILE_ROWS), :],
                              buf.at[slot, pl.ds(0, ntiles * TILE_ROWS), :], sem.at[slot]).wait()

    def staged(buf, slot, first_tile, ntiles, k):
        return (slot, pl.ds(first_tile * TILE_ROWS + k, ntiles, stride=TILE_ROWS), slice(None))

    def finish_chunk(c, dst):
        slot = c % 2
        wait_tiles(h2_hbm[0], hst, slot, lat_step, sem_h)
        if c == 0 and r_ctx:
            wait_tiles(h2_hbm[1], hst, 0, r_ctx, sem_h)
        for k in range(TILE_ROWS):
            cols = slice(k * LANES, (k + 1) * LANES)
            xe[dst, c * lat_step:(c + 1) * lat_step, cols] = hst[staged(hst, slot, 0, lat_step, k)].astype(BF16)
            if c == 0 and r_ctx:
                xe[dst, r_lat:r_lat + r_ctx, cols] = hst[staged(hst, 0, lat_step, r_ctx, k)].astype(BF16)

    for j in range(nf):
        @pl.when(f == j)
        def _land(j=j):
            if j == 0:
                @pl.when(e == 0)
                def _first_expert():
                    for c in range(nf):
                        issue_rows(idx_ref, h2_hbm[0], c * lat_step, lat_step, hst, c % 2, 0, sem_h, True, False)
                        if c == 0 and r_ctx:
                            issue_rows(idx_ref, h2_hbm[1], r_lat, r_ctx, hst, 0, lat_step, sem_h, True, False)
                        finish_chunk(c, cur)

                @pl.when(e > 0)
                def _last_chunk():
                    finish_chunk(nf - 1, cur)
                acc[...] = jnp.zeros_like(acc)
                if r_ctx:
                    issue_rows(idxn_ref, h2_hbm[1], r_lat, r_ctx, hst, 0, lat_step, sem_h, True, False)
            else:
                finish_chunk(j - 1, nxt)

    wg = wg_ref[0, 0].astype(BF16)
    wu = wu_ref[0, 0].astype(BF16)
    wd = wd_ref[0, 0].astype(BF16)
    hslot = lax.rem(f, 2)

    def ffn_rows(r0, nrows):
        xs = xe[cur, pl.ds(r0, nrows), :]
        a = _dot(xs, wg)
        u = _dot(xs, wu)
        hm = (a * jax.nn.sigmoid(a) * u).astype(BF16)
        acc[pl.ds(r0, nrows), :] += _dot(hm, wd)

    def prefetch_lat(rb):
        issue_rows(idxn_ref, h2_hbm[0], f * lat_step + rb * lat_it, lat_it, hst, hslot, rb * lat_it,
                   sem_h, True, True)

    @pl.when(f < nf - 1)
    def _plain_step():
        def body_prefetch(rb, carry):
            prefetch_lat(rb)
            ffn_rows(pl.multiple_of(rb * rblk, rblk), rblk)
            return carry

        def body(rb, carry):
            ffn_rows(pl.multiple_of(rb * rblk, rblk), rblk)
            return carry
        lax.fori_loop(0, pre_blocks, body_prefetch, 0)
        lax.fori_loop(pre_blocks, nb, body, 0)
        if r_ctx:
            ffn_rows(r_lat, r_ctx)

    @pl.when(f == nf - 1)
    def _last_step():
        blocks = [(i * rblk, rblk, 0, gt_rows[0][(i * rblk) // cap]) for i in range(nb)]
        if r_ctx:
            blocks.append((r_lat, r_ctx, 1, gt_rows[1][0]))
        last = len(blocks) - 1

        def gather_x(i, inline):
            r0, nrows, src, _ = blocks[i]
            issue_rows(idx_ref, x_hbm[src], r0, nrows, xst, i % x_slots, 0, sem_x, True, inline)

        def wait_x(i):
            _, nrows, src, _ = blocks[i]
            wait_tiles(x_hbm[src], xst, i % x_slots, nrows, sem_x)

        def update_and_scatter(i, inline):
            r0, nrows, src, gt_row = blocks[i]
            slot = i % x_slots
            wait_x(i)
            lane = lax.broadcasted_iota(jnp.int32, (nrows, LANES), 1)
            logits = _dot(xe[cur, r0:r0 + nrows, :], wr_ref[...])
            logits = jnp.where(lane < N_EXPERTS, logits, NEG_INF)
            ex = jnp.exp(logits - jnp.max(logits, axis=-1, keepdims=True))
            gval = (jnp.sum(jnp.where(lane == e, ex, 0.0), axis=-1, keepdims=True)
                    / jnp.sum(ex, axis=-1, keepdims=True))
            for k in range(TILE_ROWS):
                cols = slice(k * LANES, (k + 1) * LANES)
                ix = staged(xst, slot, 0, nrows, k)
                xst[ix] = xst[ix] + gt2_ref[gt_row:gt_row + 1, cols] * (acc[r0:r0 + nrows, cols] * gval)
            issue_rows(idx_ref, x_hbm[src], r0, nrows, xst, slot, 0, sem_x, False, inline)

        for i in range(min(ahead, last + 1)):
            gather_x(i, False)
        for i, (r0, nrows, _, _) in enumerate(blocks):
            if i >= 1:
                update_and_scatter(i - 1, True)
            if i < pre_blocks:
                prefetch_lat(i)
            if i >= 2:
                wait_x(i - 2)
            if i + ahead <= last:
                gather_x(i + ahead, True)
            ffn_rows(r0, nrows)
        update_and_scatter(last, False)
        if last >= 1:
            wait_x(last - 1)
        wait_x(last)

        @pl.when(e == ne - 1)
        def _drain():
            wait_tiles(h2_hbm[0], hst, (nf - 1) % 2, lat_step, sem_h)


def _expert_choice(groups, layer, w_gate, w_up, w_down, wr_pad, gt2):
    idx_parts, h2s, xs, counts, caps = [], [], [], [], []
    for h2, aff, x, _ in groups:
        b, n = x.shape[0], _num_tokens(x.shape)
        cap = CAPACITY_FACTOR * n // N_EXPERTS
        idx = _select_call(aff, cap)
        flat = idx + (jnp.arange(b, dtype=jnp.int32) * n)[:, None, None]
        idx_parts.append(jnp.transpose(flat, (1, 0, 2)).reshape(N_EXPERTS, b * cap))
        counts.append(b * cap)
        caps.append(cap)
        h2s.append(h2.reshape(b * n * TILE_ROWS, LANES))
        xs.append(x.reshape(b * n * TILE_ROWS, LANES))
    idx_all = jnp.concatenate(idx_parts, axis=1)
    outs = _expert_call(idx_all, h2s, xs, counts, caps[0], tuple(tuple(g[3]) for g in groups), layer,
                        w_gate, w_up, w_down, wr_pad, gt2)
    return [o.reshape(g[2].shape) for o, g in zip(outs, groups)]


def _expert_call(idx_all, h2s, xs, counts, cap, gt_rows, layer, w_gate, w_up, w_down, wr_pad, gt2):
    d = D_MODEL
    ff = w_gate.shape[-1]
    tf = FF_TILE
    nf = ff // tf
    nsrc = len(h2s)
    r_lat = counts[0]
    r_ctx = counts[1] if nsrc > 1 else 0
    rows = r_lat + r_ctx
    rblk = min(ROW_BLOCK, cap)
    idx_all = idx_all.reshape(N_EXPERTS, 1, rows)
    stage_tiles = r_lat // nf + r_ctx
    any_spec = pl.BlockSpec(memory_space=pl.ANY)
    idx_spec = lambda shift: pl.BlockSpec(
        (1, 1, rows), lambda ei, fi: (jnp.minimum(ei + shift, N_EXPERTS - 1), 0, 0), memory_space=pltpu.SMEM)
    return pl.pallas_call(
        functools.partial(_expert_kernel, nsrc=nsrc, nf=nf, ne=N_EXPERTS, r_lat=r_lat, r_ctx=r_ctx,
                          cap=cap, rblk=rblk, gt_rows=gt_rows),
        grid=(N_EXPERTS, nf),
        in_specs=[idx_spec(0), idx_spec(1)]
        + [any_spec] * (2 * nsrc)
        + [pl.BlockSpec((1, 1, d, tf), lambda ei, fi: (layer, ei, 0, fi)),
           pl.BlockSpec((1, 1, d, tf), lambda ei, fi: (layer, ei, 0, fi)),
           pl.BlockSpec((1, 1, tf, d), lambda ei, fi: (layer, ei, fi, 0)),
           pl.BlockSpec((d, LANES), lambda ei, fi: (0, 0)),
           pl.BlockSpec(gt2.shape, lambda ei, fi: (0, 0))],
        out_specs=[any_spec] * nsrc,
        out_shape=[jax.ShapeDtypeStruct(xa.shape, F32) for xa in xs],
        scratch_shapes=[pltpu.VMEM((2, rows, d), BF16), pltpu.VMEM((rows, d), F32),
                        pltpu.VMEM((2, stage_tiles * TILE_ROWS, LANES), F32),
                        pltpu.VMEM((X_SLOTS, rblk * TILE_ROWS, LANES), F32),
                        pltpu.SemaphoreType.DMA((2,)), pltpu.SemaphoreType.DMA((X_SLOTS,))],
        input_output_aliases={2 + nsrc + s: s for s in range(nsrc)},
        compiler_params=_cparams(("arbitrary", "arbitrary")),
        name="expert_ffn",
    )(idx_all, idx_all, *h2s, *xs, w_gate, w_up, w_down, wr_pad, gt2)


def _final_norm_kernel(x_ref, g_ref, o_ref):
    x = _load_tokens(x_ref)
    ms = jnp.mean(x * x, axis=-1, keepdims=True)
    o_ref[0] = x * lax.rsqrt(ms + EPS) * g_ref[...]


def _final_norm_call(x, g):
    b, n = x.shape[0], _num_tokens(x.shape)
    d = D_MODEL
    tm = min(TOKEN_TILE, n)
    return pl.pallas_call(
        _final_norm_kernel,
        grid=(b, n // tm),
        in_specs=[_token_spec(tm, x.shape),
                  pl.BlockSpec((1, d), lambda bi, i: (0, 0))],
        out_specs=pl.BlockSpec((1, tm, d), lambda bi, i: (bi, i, 0)),
        out_shape=jax.ShapeDtypeStruct((b, n, d), F32),
        compiler_params=_cparams(("parallel", "arbitrary")),
        name="final_norm",
    )(x, g)


def _rope_tables(n):
    rows = n // GRID_W
    row = np.repeat(np.arange(rows, dtype=np.float32), GRID_W)
    col = np.tile(np.arange(GRID_W, dtype=np.float32), rows)
    pairs = HEAD_DIM // 4
    inv = (np.float32(ROPE_BASE) ** (-np.arange(pairs, dtype=np.float32) / pairs)).astype(np.float32)
    ang = np.concatenate([row[:, None] * inv, col[:, None] * inv], axis=-1).astype(np.float32)
    cos, sin = np.cos(ang), np.sin(ang)
    cos_t = np.concatenate([cos, cos, cos, cos], axis=-1)
    sin_t = np.concatenate([-sin, sin, -sin, sin], axis=-1)
    return jnp.asarray(cos_t, F32), jnp.asarray(sin_t, F32)


def _band_bias(l_ctx):
    qi = np.arange(BLOCK)[:, None]
    kj = np.arange(3 * BLOCK)[None, :]
    ok = np.abs(kj - BLOCK - qi) <= WINDOW
    bias = np.concatenate([np.zeros((BLOCK, l_ctx)), np.where(ok, 0.0, NEG_INF)], axis=1).astype(np.float32)
    return jnp.asarray(np.concatenate([bias, bias], axis=0))


def kernel(x, c, ctx, c_ctx, w_mod, b_mod, g_mix, g_ffn, w_in, attn_sink, w_spatial, b_spatial,
           w_branch_attn, w_branch_sgu, w_branch_fourier, w_out, w_router, w_gate, w_up, w_down, g_final):
    b, n, d = x.shape
    l_ctx = ctx.shape[1]
    depth = w_mod.shape[0]

    cvec = jnp.zeros((SUBLANES, d), F32).at[0:b].set(c).at[b].set(c_ctx)
    mod = _mod_call(cvec, w_mod, b_mod).reshape(depth, SUBLANES, N_MOD, d)
    ctx_row = b

    cos_t, sin_t = _rope_tables(n)
    bias = _band_bias(l_ctx)
    no_rope = jnp.zeros((l_ctx, LANES), F32)
    gsz = SGU_WIDTH // SGU_GROUPS
    bd = jnp.asarray(np.kron(np.eye(SGU_GROUPS), np.ones((gsz, gsz))) / gsz, BF16)

    for layer in range(depth):
        last = layer == depth - 1
        w_l = w_in[layer].astype(BF16)
        g1 = g_mix[layer].reshape(1, d)
        g2 = g_ffn[layer].reshape(1, d)
        wsp = jnp.transpose(w_spatial[layer], (1, 0, 2)).reshape(CHUNK, SGU_GROUPS * CHUNK).astype(BF16)
        bsp = jnp.repeat(jnp.transpose(b_spatial[layer]), gsz, axis=1)
        wba = w_branch_attn[layer].astype(BF16)
        wbs = w_branch_sgu[layer].astype(BF16)
        wbf = w_branch_fourier[layer].astype(BF16)
        wo = w_out[layer].astype(BF16)
        wr_t = jnp.transpose(w_router[layer]).astype(BF16)
        wr_pad = jnp.pad(w_router[layer], ((0, 0), (0, LANES - N_EXPERTS))).astype(BF16)
        sink = attn_sink[layer]
        mod_l = mod[layer]

        qc, kcd, vcd, uc, vsc, fc, gc = _proj_call(ctx, mod_l, ctx_row, g1, w_l, no_rope, no_rope, rope=False)
        if not last:
            a_c = _attn_call(qc, None, None, kcd, vcd, sink, None, local=False)
            f_c = _fourier_call(fc)
            ctx_mid, h2c, aff_c = _merge_call(a_c, uc, vsc, f_c, gc, ctx, mod_l, ctx_row, g2,
                                              wsp, bsp, bd, wba, wbs, wbf, wo, wr_t)

        q, kd, vd, u, vs, f, gates = _proj_call(x, mod_l, None, g1, w_l, cos_t, sin_t, rope=True)
        a = _attn_call(q, kd, vd, kcd, vcd, sink, bias, local=True)
        fo = _fourier_call(f)
        x_mid, h2, aff = _merge_call(a, u, vs, fo, gates, x, mod_l, None, g2,
                                     wsp, bsp, bd, wba, wbs, wbf, wo, wr_t)

        gt2 = mod_l[:, 5]
        groups = [(h2, aff, x_mid, list(range(b)))]
        if not last:
            groups.append((h2c, aff_c, ctx_mid, [ctx_row] * b))
        outs = _expert_choice(groups, layer, w_gate, w_up, w_down, wr_pad, gt2)
        x = outs[0]
        if not last:
            ctx = outs[1]
    return _final_norm_call(x, g_final.reshape(1, d))
```

```python
import functools
import math

import numpy as np
import jax
import jax.numpy as jnp
from jax import lax
from jax.experimental import pallas as pl
from jax.experimental.pallas import tpu as pltpu

F32 = jnp.float32
BF16 = jnp.bfloat16

D_MODEL = 1024
GRID_W = 64
HEAD_DIM = 64
N_HEADS = 8
N_KV_HEADS = 2
ATTN_WIDTH = N_HEADS * HEAD_DIM
KV_WIDTH = N_KV_HEADS * HEAD_DIM
WINDOW = 128
BLOCK = 128
ROPE_BASE = 10000.0
SGU_GROUPS = 4
SGU_WIDTH = 256
CHUNK = 128
FNET_GROUPS = 4
FNET_GROUP_DIM = 64
FNET_WIDTH = 256
N_EXPERTS = 16
CAPACITY_FACTOR = 2
EXPERT_FF = 2048
N_MOD = 6
EPS = 1e-6
NEG_INF = -1e30
OFF_KV = ATTN_WIDTH
OFF_U = ATTN_WIDTH + 2 * KV_WIDTH
OFF_G = OFF_U + 2 * SGU_WIDTH + FNET_WIDTH
IN_WIDTH = OFF_G + 3 * D_MODEL

LANES = 128
SUBLANES = 8
VMEM_LIMIT_BYTES = 56 * 1024 * 1024

TOKEN_TILE = 512
ATTN_TILE = 256
SEL_ROWS = 128
FF_TILE = 256
ROW_BLOCK = 512
DMA_UNROLL = 8
X_SLOTS = 3


def _cparams(sem):
    return pltpu.CompilerParams(dimension_semantics=sem, vmem_limit_bytes=VMEM_LIMIT_BYTES)


def _dot(a, b):
    return jnp.dot(a, b, preferred_element_type=F32)


def _dot_t(a, b):
    return lax.dot_general(a, b, (((1,), (1,)), ((), ())), preferred_element_type=F32)


def _mod_kernel(c_ref, w_ref, b_ref, o_ref):
    c = c_ref[...]
    s = (c * jax.nn.sigmoid(c)).astype(BF16)
    o_ref[0] = _dot(s, w_ref[0].astype(BF16)) + b_ref[0]


def _mod_call(cvec, w_mod, b_mod):
    depth, d, width = w_mod.shape
    tn = 1536
    return pl.pallas_call(
        _mod_kernel,
        grid=(depth, width // tn),
        in_specs=[
            pl.BlockSpec((SUBLANES, d), lambda l, j: (0, 0)),
            pl.BlockSpec((1, d, tn), lambda l, j: (l, 0, j)),
            pl.BlockSpec((1, 1, tn), lambda l, j: (l, 0, j)),
        ],
        out_specs=pl.BlockSpec((1, SUBLANES, tn), lambda l, j: (l, 0, j)),
        out_shape=jax.ShapeDtypeStruct((depth, SUBLANES, width), F32),
        compiler_params=_cparams(("arbitrary", "arbitrary")),
        name="adaln_mod",
    )(cvec, w_mod, b_mod.reshape(depth, 1, width))


def _swap_halves(t):
    lane = lax.broadcasted_iota(jnp.int32, t.shape, 1)
    first = (lane % HEAD_DIM) < (HEAD_DIM // 2)
    return jnp.where(first, pltpu.roll(t, LANES - HEAD_DIM // 2, 1), pltpu.roll(t, HEAD_DIM // 2, 1))


def _dup_heads(t):
    lane = lax.broadcasted_iota(jnp.int32, t.shape, 1)
    lo = lane < HEAD_DIM
    r = pltpu.roll(t, HEAD_DIM, 1)
    return jnp.concatenate([jnp.where(lo, t, r), jnp.where(lo, r, t)], axis=1)


TILE_ROWS = D_MODEL // LANES


def _is_row_layout(shape):
    return shape[-1] == D_MODEL


def _num_tokens(shape):
    return shape[1] if _is_row_layout(shape) else shape[1] // TILE_ROWS


def _load_tokens(ref):
    if _is_row_layout(ref.shape):
        return ref[0]
    tm = ref.shape[1] // TILE_ROWS
    return jnp.concatenate([ref[0, pl.ds(k, tm, stride=TILE_ROWS), :] for k in range(TILE_ROWS)], axis=1)


def _store_tokens(ref, val):
    if _is_row_layout(ref.shape):
        ref[0] = val
    else:
        tm = ref.shape[1] // TILE_ROWS
        for k in range(TILE_ROWS):
            ref[0, pl.ds(k, tm, stride=TILE_ROWS), :] = val[:, k * LANES:(k + 1) * LANES]


def _token_spec(tm, shape):
    rows = tm if _is_row_layout(shape) else tm * TILE_ROWS
    return pl.BlockSpec((1, rows, shape[2]), lambda bi, i: (bi, i, 0))


def _proj_kernel(x_ref, m_ref, g_ref, w_ref, cos_ref, sin_ref,
                 q_ref, k_ref, v_ref, u_ref, vs_ref, f_ref, gt_ref, *, rope):
    x = _load_tokens(x_ref)
    ms = jnp.mean(x * x, axis=-1, keepdims=True)
    y = x * lax.rsqrt(ms + EPS) * g_ref[...]
    h = y * (1.0 + m_ref[0, 1:2, :]) + m_ref[0, 0:1, :]
    hb = h.astype(BF16)

    if rope:
        cos = cos_ref[...]
        sin = sin_ref[...]

    def rot(t):
        if not rope:
            return t
        return t * cos + _swap_halves(t) * sin

    q = _dot(hb, w_ref[:, 0:OFF_KV])
    for j in range(ATTN_WIDTH // LANES):
        qj = rot(q[:, j * LANES:(j + 1) * LANES]) * (HEAD_DIM ** -0.5)
        q_ref[0, :, j * LANES:(j + 1) * LANES] = qj.astype(BF16)
    kv = _dot(hb, w_ref[:, OFF_KV:OFF_U])
    k_ref[0] = _dup_heads(rot(kv[:, 0:KV_WIDTH])).astype(BF16)
    v_ref[0] = _dup_heads(kv[:, KV_WIDTH:2 * KV_WIDTH]).astype(BF16)
    uvf = _dot(hb, w_ref[:, OFF_U:OFF_G])
    u_ref[0] = uvf[:, 0:SGU_WIDTH].astype(BF16)
    vs_ref[0] = uvf[:, SGU_WIDTH:2 * SGU_WIDTH].astype(BF16)
    f_ref[0] = uvf[:, 2 * SGU_WIDTH:].astype(BF16)
    for j in range(3):
        gj = _dot(hb, w_ref[:, OFF_G + j * D_MODEL:OFF_G + (j + 1) * D_MODEL])
        gt_ref[0, :, j * D_MODEL:(j + 1) * D_MODEL] = gj.astype(BF16)


def _proj_call(x, mod, mod_row, g, w, cos_t, sin_t, *, rope):
    b, n = x.shape[0], _num_tokens(x.shape)
    d = D_MODEL
    tm = min(TOKEN_TILE, n)
    if mod_row is None:
        mod_map = lambda bi, i: (bi, 0, 0)
    else:
        mod_map = lambda bi, i: (mod_row, 0, 0)
    widths = [ATTN_WIDTH, 2 * KV_WIDTH, 2 * KV_WIDTH, SGU_WIDTH, SGU_WIDTH, FNET_WIDTH, 3 * D_MODEL]
    return pl.pallas_call(
        functools.partial(_proj_kernel, rope=rope),
        grid=(b, n // tm),
        in_specs=[
            _token_spec(tm, x.shape),
            pl.BlockSpec((1, N_MOD, d), mod_map),
            pl.BlockSpec((1, d), lambda bi, i: (0, 0)),
            pl.BlockSpec((d, IN_WIDTH), lambda bi, i: (0, 0)),
            pl.BlockSpec((tm, LANES), lambda bi, i: (i, 0)),
            pl.BlockSpec((tm, LANES), lambda bi, i: (i, 0)),
        ],
        out_specs=[pl.BlockSpec((1, tm, wd), lambda bi, i: (bi, i, 0)) for wd in widths],
        out_shape=[jax.ShapeDtypeStruct((b, n, wd), BF16) for wd in widths],
        compiler_params=_cparams(("parallel", "arbitrary")),
        name="norm_in_proj",
    )(x, mod, g, w, cos_t, sin_t)


def _attn_kernel(sink_ref, q_ref, *rest, nsub, local):
    if local:
        kp_ref, ko_ref, kn_ref, vp_ref, vo_ref, vn_ref, kc_ref, vc_ref, bias_ref, o_ref = rest
        l_ctx = kc_ref.shape[1]
        i = pl.program_id(1)
        last = pl.num_programs(1) - 1
        kcat = jnp.concatenate([kp_ref[0], ko_ref[0], kn_ref[0]], axis=0)
        vcat = jnp.concatenate([vp_ref[0], vo_ref[0], vn_ref[0]], axis=0)
        bias = bias_ref[...]
        col = lax.broadcasted_iota(jnp.int32, (1, l_ctx + 3 * BLOCK), 1)
        lo_cols = jnp.logical_and(col >= l_ctx, col < l_ctx + BLOCK).astype(F32)
        hi_cols = (col >= l_ctx + 2 * BLOCK).astype(F32)
        edge_lo = jnp.where(i == 0, NEG_INF, 0.0)
        edge_hi = jnp.where(i == last, NEG_INF, 0.0)
    else:
        kc_ref, vc_ref, o_ref = rest
    kc = kc_ref[0]
    vc = vc_ref[0]
    lane_k = lax.broadcasted_iota(jnp.int32, (1, LANES), 1)
    even_k = lane_k < HEAD_DIM
    zero = jnp.zeros((), BF16)
    lane_o = lax.broadcasted_iota(jnp.int32, (BLOCK, LANES), 1)

    for j in range(nsub):
        r0 = j * BLOCK
        pairs = []
        for g in range(N_KV_HEADS):
            c0 = g * 2 * LANES
            lhs = jnp.concatenate([q_ref[0, r0:r0 + BLOCK, c0:c0 + LANES],
                                   q_ref[0, r0:r0 + BLOCK, c0 + LANES:c0 + 2 * LANES]], axis=0)
            keys = kc[:, g * LANES:(g + 1) * LANES]
            vals = vc[:, g * LANES:(g + 1) * LANES]
            if local:
                keys = jnp.concatenate([keys, kcat[r0:r0 + 3 * BLOCK, g * LANES:(g + 1) * LANES]], axis=0)
                vals = jnp.concatenate([vals, vcat[r0:r0 + 3 * BLOCK, g * LANES:(g + 1) * LANES]], axis=0)
                bj = bias
                if j == 0:
                    bj = bj + lo_cols * edge_lo
                if j == nsub - 1:
                    bj = bj + hi_cols * edge_hi
            halves = []
            for par in range(2):
                sel = even_k if par == 0 else jnp.logical_not(even_k)
                s = _dot_t(lhs, jnp.where(sel, keys, zero))
                if local:
                    s = s + bj
                sk = jnp.concatenate([jnp.full((BLOCK, 1), sink_ref[4 * g + par], F32),
                                      jnp.full((BLOCK, 1), sink_ref[4 * g + 2 + par], F32)], axis=0)
                m = jnp.maximum(jnp.max(s, axis=-1, keepdims=True), sk)
                p = jnp.exp(s - m)
                den = jnp.sum(p, axis=-1, keepdims=True) + jnp.exp(sk - m)
                halves.append(_dot(p.astype(BF16), vals) / den)
            for pr in range(2):
                pairs.append(jnp.where(lane_o < HEAD_DIM,
                                       halves[0][pr * BLOCK:(pr + 1) * BLOCK],
                                       halves[1][pr * BLOCK:(pr + 1) * BLOCK]))
        o_ref[0, r0:r0 + BLOCK, :] = jnp.concatenate(pairs, axis=1).astype(BF16)


def _attn_call(q, kd, vd, kcd, vcd, sink, bias, *, local):
    b, n, _ = q.shape
    l_ctx = kcd.shape[1]
    tq = min(ATTN_TILE, n)
    nsub = tq // BLOCK
    nblk = n // BLOCK
    kw = 2 * KV_WIDTH
    in_specs = [pl.BlockSpec(memory_space=pltpu.SMEM),
                pl.BlockSpec((1, tq, ATTN_WIDTH), lambda bi, i: (bi, i, 0))]
    args = [sink, q]
    if local:
        prev_map = lambda bi, i: (bi, jnp.maximum(i * nsub - 1, 0), 0)
        own_map = lambda bi, i: (bi, i, 0)
        next_map = lambda bi, i: (bi, jnp.minimum((i + 1) * nsub, nblk - 1), 0)
        for arr in (kd, vd):
            in_specs += [pl.BlockSpec((1, BLOCK, kw), prev_map),
                         pl.BlockSpec((1, tq, kw), own_map),
                         pl.BlockSpec((1, BLOCK, kw), next_map)]
            args += [arr, arr, arr]
    in_specs += [pl.BlockSpec((1, l_ctx, kw), lambda bi, i: (bi, 0, 0)),
                 pl.BlockSpec((1, l_ctx, kw), lambda bi, i: (bi, 0, 0))]
    args += [kcd, vcd]
    if local:
        in_specs.append(pl.BlockSpec((2 * BLOCK, l_ctx + 3 * BLOCK), lambda bi, i: (0, 0)))
        args.append(bias)
    return pl.pallas_call(
        functools.partial(_attn_kernel, nsub=nsub, local=local),
        grid=(b, n // tq),
        in_specs=in_specs,
        out_specs=pl.BlockSpec((1, tq, ATTN_WIDTH), lambda bi, i: (bi, i, 0)),
        out_shape=jax.ShapeDtypeStruct((b, n, ATTN_WIDTH), BF16),
        compiler_params=_cparams(("parallel", "arbitrary")),
        name="window_attention" if local else "context_attention",
    )(*args)


def _fft1_kernel(w_ref, x_ref, z_ref):
    z_ref[0] = _dot(w_ref[...].astype(BF16), x_ref[0]).astype(BF16)


def _fft2_kernel(t_ref, z_ref, cs_ref, o_ref, *, kb):
    n2 = z_ref.shape[3]
    cs = cs_ref[...].astype(BF16)
    for j in range(kb):
        zk = jnp.concatenate([z_ref[0, 0, j], z_ref[0, 1, j]], axis=0)
        y = _dot(t_ref[j].astype(BF16), zk)
        yc = jnp.concatenate([y[0:n2], y[n2:2 * n2]], axis=1).astype(BF16)
        o_ref[0, :, j, :] = _dot(yc, cs).astype(BF16)


def _fft_dense_kernel(cn_ref, sn_ref, cc_ref, sc_ref, x_ref, o_ref):
    x = x_ref[0]
    a = _dot(x, cc_ref[...].astype(BF16)).astype(BF16)
    bm = _dot(x, sc_ref[...].astype(BF16)).astype(BF16)
    o_ref[0] = (_dot(cn_ref[...].astype(BF16), a) - _dot(sn_ref[...].astype(BF16), bm)).astype(BF16)


def _channel_dft():
    c = np.arange(FNET_GROUP_DIM)
    ang = 2.0 * np.pi * np.outer(c, c) / FNET_GROUP_DIM
    eye = np.eye(FNET_GROUPS)
    cc = np.kron(eye, np.cos(ang)) / math.sqrt(FNET_GROUP_DIM)
    sc = np.kron(eye, np.sin(ang)) / math.sqrt(FNET_GROUP_DIM)
    return cc, sc


@functools.lru_cache(maxsize=None)
def _fft_tables(n):
    n2 = LANES
    n1 = n // n2
    t1 = np.arange(n1)
    a1 = 2.0 * np.pi * np.outer(t1, t1) / n1
    w1 = np.concatenate([np.cos(a1), -np.sin(a1)], axis=0) / math.sqrt(n)
    k = np.arange(n1)[:, None, None] + n1 * np.arange(n2)[None, :, None]
    a2 = 2.0 * np.pi * k * np.arange(n2)[None, None, :] / n
    mr, mi = np.cos(a2), -np.sin(a2)
    tb = np.concatenate([np.concatenate([mr, -mi], axis=2),
                         np.concatenate([mi, mr], axis=2)], axis=1)
    cc, sc = _channel_dft()
    cs = np.concatenate([cc, sc], axis=0)
    return (jnp.asarray(w1, F32), jnp.asarray(tb, F32), jnp.asarray(cs, F32))


@functools.lru_cache(maxsize=None)
def _fft_dense_tables(n):
    t = np.arange(n)
    a = 2.0 * np.pi * np.outer(t, t) / n
    cc, sc = _channel_dft()
    return (jnp.asarray(np.cos(a) / math.sqrt(n), F32), jnp.asarray(np.sin(a) / math.sqrt(n), F32),
            jnp.asarray(cc, F32), jnp.asarray(sc, F32))


def _fourier_call(f):
    b, n, c = f.shape
    if n <= 2 * LANES:
        cn, sn, cc, sc = _fft_dense_tables(n)
        full = lambda bi: (0, 0)
        return pl.pallas_call(
            _fft_dense_kernel,
            grid=(b,),
            in_specs=[pl.BlockSpec((n, n), full), pl.BlockSpec((n, n), full),
                      pl.BlockSpec((c, c), full), pl.BlockSpec((c, c), full),
                      pl.BlockSpec((1, n, c), lambda bi: (bi, 0, 0))],
            out_specs=pl.BlockSpec((1, n, c), lambda bi: (bi, 0, 0)),
            out_shape=jax.ShapeDtypeStruct((b, n, c), BF16),
            compiler_params=_cparams(("parallel",)),
            name="fourier_dense",
        )(cn, sn, cc, sc, f)
    n2 = LANES
    n1 = n // n2
    w1, tb, cs = _fft_tables(n)
    cols = n2 * c
    tc = 2048
    z = pl.pallas_call(
        _fft1_kernel,
        grid=(b, cols // tc),
        in_specs=[pl.BlockSpec((2 * n1, n1), lambda bi, j: (0, 0)),
                  pl.BlockSpec((1, n1, tc), lambda bi, j: (bi, 0, j))],
        out_specs=pl.BlockSpec((1, 2 * n1, tc), lambda bi, j: (bi, 0, j)),
        out_shape=jax.ShapeDtypeStruct((b, 2 * n1, cols), BF16),
        compiler_params=_cparams(("parallel", "arbitrary")),
        name="fourier_stage1",
    )(w1, f.reshape(b, n1, cols))
    kb = SUBLANES
    out = pl.pallas_call(
        functools.partial(_fft2_kernel, kb=kb),
        grid=(n1 // kb, b),
        in_specs=[pl.BlockSpec((kb, 2 * n2, 2 * n2), lambda j, bi: (j, 0, 0)),
                  pl.BlockSpec((1, 2, kb, n2, c), lambda j, bi: (bi, 0, j, 0, 0)),
                  pl.BlockSpec((2 * c, c), lambda j, bi: (0, 0))],
        out_specs=pl.BlockSpec((1, n2, kb, c), lambda j, bi: (bi, 0, j, 0)),
        out_shape=jax.ShapeDtypeStruct((b, n2, n1, c), BF16),
        compiler_params=_cparams(("arbitrary", "arbitrary")),
        name="fourier_stage2",
    )(tb, z.reshape(b, 2, n1, n2, c), cs)
    return out.reshape(b, n, c)


def _merge_kernel(a_ref, u_ref, vs_ref, fo_ref, gt_ref, x_ref, m_ref, g2_ref,
                  wsp_ref, bsp_ref, bd_ref, wba_ref, wbs_ref, wbf_ref, wo_ref, wr_ref,
                  xo_ref, h2_ref, aff_ref, s_scr):
    tm = a_ref.shape[1]
    lane = lax.broadcasted_iota(jnp.int32, (CHUNK, SGU_WIDTH), 1)
    for c in range(tm // CHUNK):
        r0 = c * CHUNK
        ug = jax.nn.gelu(u_ref[0, r0:r0 + CHUNK, :].astype(F32))
        vg = jax.nn.gelu(vs_ref[0, r0:r0 + CHUNK, :].astype(F32))
        sq = vg * vg
        sq_hi = sq.astype(BF16)
        sq_lo = (sq - sq_hi.astype(F32)).astype(BF16)
        msq = _dot(sq_hi, bd_ref[...]) + _dot(sq_lo, bd_ref[...])
        vn = (vg * lax.rsqrt(msq + EPS)).astype(BF16)
        zero = jnp.zeros((), BF16)
        stack = jnp.concatenate(
            [jnp.where(lax.shift_right_logical(lane, 6) == g, vn, zero) for g in range(SGU_GROUPS)], axis=0)
        z = _dot(wsp_ref[...], stack) + bsp_ref[...]
        s_scr[r0:r0 + CHUNK, :] = (ug * z).astype(BF16)

    gates = gt_ref[0]
    acc = jax.nn.sigmoid(gates[:, 0:D_MODEL].astype(F32)) * _dot(a_ref[0], wba_ref[...])
    acc = acc + jax.nn.sigmoid(gates[:, D_MODEL:2 * D_MODEL].astype(F32)) * _dot(s_scr[...], wbs_ref[...])
    acc = acc + jax.nn.sigmoid(gates[:, 2 * D_MODEL:].astype(F32)) * _dot(fo_ref[0], wbf_ref[...])
    o = _dot(acc.astype(BF16), wo_ref[...])
    xn = _load_tokens(x_ref) + m_ref[0, 2:3, :] * o
    _store_tokens(xo_ref, xn)
    ms = jnp.mean(xn * xn, axis=-1, keepdims=True)
    y = xn * lax.rsqrt(ms + EPS) * g2_ref[...]
    h2 = y * (1.0 + m_ref[0, 4:5, :]) + m_ref[0, 3:4, :]
    _store_tokens(h2_ref, h2)
    logits = _dot_t(wr_ref[...], h2.astype(BF16))
    mx = jnp.max(logits, axis=0, keepdims=True)
    ex = jnp.exp(logits - mx)
    aff_ref[0] = ex / jnp.sum(ex, axis=0, keepdims=True)


def _merge_call(a, u, vs, fo, gates, x, mod, mod_row, g2, wsp, bsp, bd, wba, wbs, wbf, wo, wr_t):
    b, n = x.shape[0], _num_tokens(x.shape)
    d = D_MODEL
    tm = min(TOKEN_TILE, n)
    if mod_row is None:
        mod_map = lambda bi, i: (bi, 0, 0)
    else:
        mod_map = lambda bi, i: (mod_row, 0, 0)
    tok = lambda wd: pl.BlockSpec((1, tm, wd), lambda bi, i: (bi, i, 0))
    full = lambda arr: pl.BlockSpec(arr.shape, lambda bi, i: (0,) * arr.ndim)
    tiled = (b, n * TILE_ROWS, LANES)
    return pl.pallas_call(
        _merge_kernel,
        grid=(b, n // tm),
        in_specs=[tok(ATTN_WIDTH), tok(SGU_WIDTH), tok(SGU_WIDTH), tok(FNET_WIDTH), tok(3 * D_MODEL),
                  _token_spec(tm, x.shape),
                  pl.BlockSpec((1, N_MOD, d), mod_map), full(g2),
                  full(wsp), full(bsp), full(bd), full(wba), full(wbs), full(wbf), full(wo), full(wr_t)],
        out_specs=[_token_spec(tm, tiled), _token_spec(tm, tiled),
                   pl.BlockSpec((1, N_EXPERTS, tm), lambda bi, i: (bi, 0, i))],
        out_shape=[jax.ShapeDtypeStruct(tiled, F32), jax.ShapeDtypeStruct(tiled, F32),
                   jax.ShapeDtypeStruct((b, N_EXPERTS, n), F32)],
        scratch_shapes=[pltpu.VMEM((tm, SGU_WIDTH), BF16)],
        compiler_params=_cparams(("parallel", "arbitrary")),
        name="merge_norm_router",
    )(a, u, vs, fo, gates, x, mod, g2, wsp, bsp, bd, wba, wbs, wbf, wo, wr_t)


def _select_kernel(aff_ref, tri_ref, lstrict_ref, idx_ref, *, cap, slots):
    rows = SEL_ROWS
    ones = jnp.ones((LANES, LANES), BF16)
    tri = tri_ref[...]
    lstrict = lstrict_ref[...]
    lane_r = lax.broadcasted_iota(jnp.int32, (rows, LANES), 1)
    row_r = lax.broadcasted_iota(jnp.int32, (rows, LANES), 0)
    eye = lane_r == row_r

    def total(mask_i32):
        return jnp.sum(jnp.sum(mask_i32, axis=0, keepdims=True), axis=1, keepdims=True)

    def cumsum(mask):
        mb = jnp.where(mask, 1.0, 0.0).astype(BF16)
        within = _dot(mb, tri)
        tot = jnp.broadcast_to(within[:, LANES - 1:LANES], (rows, LANES)).astype(BF16)
        return within + _dot(lstrict, tot)

    def split(v):
        hi = jnp.floor(v * (1.0 / 256.0))
        return hi.astype(BF16), (v - hi * 256.0).astype(BF16)

    aff_all = [aff_ref[0, e] for e in range(N_EXPERTS)]

    def as_f32(bits):
        return lax.bitcast_convert_type(bits, F32)

    def bit_step(t, prefixes):
        bit = lax.shift_left(jnp.int32(1), 30 - t)
        out = []
        for e in range(N_EXPERTS):
            cand = prefixes[e] | bit
            cnt = total((aff_all[e] >= as_f32(cand)).astype(jnp.int32))
            out.append(jnp.where(cnt >= cap, cand, prefixes[e]))
        return tuple(out)

    thr = lax.fori_loop(0, 31, bit_step, tuple(jnp.zeros((1, 1), jnp.int32) for _ in range(N_EXPERTS)))

    slot = lax.broadcasted_iota(jnp.int32, (slots, LANES), 0).astype(F32)
    lane_s = lax.broadcasted_iota(jnp.int32, (slots, LANES), 1).astype(F32)
    for e in range(N_EXPERTS):
        gt = aff_all[e] >= as_f32(thr[e] + 1)
        eq = jnp.logical_and(aff_all[e] >= as_f32(thr[e]), jnp.logical_not(gt))
        need = (cap - total(gt.astype(jnp.int32))).astype(F32)
        eq_rank = cumsum(eq) - jnp.where(eq, 1.0, 0.0)
        sel = jnp.logical_or(gt, jnp.logical_and(eq, eq_rank < need))
        gcum = cumsum(sel)
        ends = jnp.where(eye, jnp.broadcast_to(gcum[:, LANES - 1:LANES], (rows, LANES)), 0.0)
        e_hi, e_lo = split(ends)
        ones_s = jnp.ones((slots, rows), BF16)
        ends_row = _dot(ones_s, e_hi) * 256.0 + _dot(ones_s, e_lo)
        row_of = _dot(jnp.where(ends_row <= slot, 1.0, 0.0).astype(BF16), ones)
        onehot = jnp.where(lane_s == row_of, 1.0, 0.0).astype(BF16)
        g_hi, g_lo = split(gcum)
        grow = _dot(onehot, g_hi) * 256.0 + _dot(onehot, g_lo)
        lane_of = _dot(jnp.where(grow <= slot, 1.0, 0.0).astype(BF16), ones)
        token = (row_of * float(LANES) + lane_of).astype(jnp.int32)
        for blk in range(slots // LANES):
            tb = token[blk * LANES:(blk + 1) * LANES, :]
            idx_ref[0, e, blk:blk + 1, :] = jnp.sum(jnp.where(eye, tb, 0), axis=0, keepdims=True)


def _select_call(aff, cap):
    b, e, n = aff.shape
    full = SEL_ROWS * LANES
    if n < full:
        aff = jnp.pad(aff, ((0, 0), (0, 0), (0, full - n)), constant_values=-1.0)
    slots = max(LANES, cap)
    r = np.arange(LANES)
    tri = jnp.asarray(r[:, None] <= r[None, :], BF16)
    lstrict = jnp.asarray(r[None, :] < r[:, None], BF16)
    idx = pl.pallas_call(
        functools.partial(_select_kernel, cap=cap, slots=slots),
        grid=(b,),
        in_specs=[pl.BlockSpec((1, e, SEL_ROWS, LANES), lambda bi: (bi, 0, 0, 0)),
                  pl.BlockSpec((LANES, LANES), lambda bi: (0, 0)),
                  pl.BlockSpec((LANES, LANES), lambda bi: (0, 0))],
        out_specs=pl.BlockSpec((1, e, slots // LANES, LANES), lambda bi: (bi, 0, 0, 0)),
        out_shape=jax.ShapeDtypeStruct((b, e, slots // LANES, LANES), jnp.int32),
        compiler_params=_cparams(("parallel",)),
        name="expert_choice_select",
    )(aff.reshape(b, e, SEL_ROWS, LANES), tri, lstrict)
    return idx.reshape(b, e, slots)[:, :, :cap]


def _expert_kernel(idx_ref, idxn_ref, *refs, nsrc, nf, ne, r_lat, r_ctx, cap, rblk, gt_rows):
    h2_hbm = refs[0:nsrc]
    wg_ref, wu_ref, wd_ref, wr_ref, gt2_ref = refs[2 * nsrc:2 * nsrc + 5]
    x_hbm = refs[2 * nsrc + 5:3 * nsrc + 5]
    xe, acc, hst, xst, sem_h, sem_x = refs[3 * nsrc + 5:]
    e = pl.program_id(0)
    f = pl.program_id(1)
    cur = lax.rem(e, 2)
    nxt = 1 - cur
    nb = r_lat // rblk
    lat_step = r_lat // nf
    lat_it = lat_step // nb
    assert lat_step * nf == r_lat and lat_it * nb == lat_step and lat_step % DMA_UNROLL == 0
    assert r_ctx % DMA_UNROLL == 0 and rblk % DMA_UNROLL == 0 and cap % rblk == 0
    x_slots = xst.shape[0]
    ahead = x_slots - 2

    def tile_dma(hbm, tok, buf, slot, pos, sem, to_vmem, prio):
        h_tile = hbm.at[pl.ds(pl.multiple_of(tok * TILE_ROWS, TILE_ROWS), TILE_ROWS), :]
        v_tile = buf.at[slot, pl.ds(pl.multiple_of(pos * TILE_ROWS, TILE_ROWS), TILE_ROWS), :]
        if to_vmem:
            pltpu.make_async_copy(h_tile, v_tile, sem.at[slot]).start(priority=prio)
        else:
            pltpu.make_async_copy(v_tile, h_tile, sem.at[slot]).start(priority=prio)

    def issue_rows(iref, hbm, first, count, buf, slot, pos0, sem, to_vmem, inline):
        if inline:
            for s in range(count):
                tile_dma(hbm, iref[0, 0, first + s], buf, slot, pos0 + s, sem, to_vmem, s % 2)
            return

        def body(t, carry):
            for s in range(DMA_UNROLL):
                r = t * DMA_UNROLL + s
                tile_dma(hbm, iref[0, 0, first + r], buf, slot, pos0 + r, sem, to_vmem, s % 2)
            return carry
        lax.fori_loop(0, count // DMA_UNROLL, body, 0)

    def wait_tiles(hbm, buf, slot, ntiles, sem):
        pltpu.make_async_copy(hbm.at[pl.ds(0, ntiles * TILE_ROWS), :],
                              buf.at[slot, pl.ds(0, ntiles * TILE_ROWS), :], sem.at[slot]).wait()

    def staged(buf, slot, first_tile, ntiles, k):
        return (slot, pl.ds(first_tile * TILE_ROWS + k, ntiles, stride=TILE_ROWS), slice(None))

    def finish_chunk(c, dst):
        slot = c % 2
        wait_tiles(h2_hbm[0], hst, slot, lat_step, sem_h)
        if c == 0 and r_ctx:
            wait_tiles(h2_hbm[1], hst, 0, r_ctx, sem_h)
        for k in range(TILE_ROWS):
            cols = slice(k * LANES, (k + 1) * LANES)
            xe[dst, c * lat_step:(c + 1) * lat_step, cols] = hst[staged(hst, slot, 0, lat_step, k)].astype(BF16)
            if c == 0 and r_ctx:
                xe[dst, r_lat:r_lat + r_ctx, cols] = hst[staged(hst, 0, lat_step, r_ctx, k)].astype(BF16)

    for j in range(nf):
        @pl.when(f == j)
        def _land(j=j):
            if j == 0:
                @pl.when(e == 0)
                def _first_expert():
                    for c in range(nf):
                        issue_rows(idx_ref, h2_hbm[0], c * lat_step, lat_step, hst, c % 2, 0, sem_h, True, False)
                        if c == 0 and r_ctx:
                            issue_rows(idx_ref, h2_hbm[1], r_lat, r_ctx, hst, 0, lat_step, sem_h, True, False)
                        finish_chunk(c, cur)

                @pl.when(e > 0)
                def _last_chunk():
                    finish_chunk(nf - 1, cur)
                acc[...] = jnp.zeros_like(acc)
                if r_ctx:
                    issue_rows(idxn_ref, h2_hbm[1], r_lat, r_ctx, hst, 0, lat_step, sem_h, True, False)
            else:
                finish_chunk(j - 1, nxt)

    wg = wg_ref[0, 0].astype(BF16)
    wu = wu_ref[0, 0].astype(BF16)
    wd = wd_ref[0, 0].astype(BF16)
    hslot = lax.rem(f, 2)

    def ffn_rows(r0, nrows):
        xs = xe[cur, pl.ds(r0, nrows), :]
        a = _dot(xs, wg)
        u = _dot(xs, wu)
        hm = (a * jax.nn.sigmoid(a) * u).astype(BF16)
        acc[pl.ds(r0, nrows), :] += _dot(hm, wd)

    def prefetch_lat(rb):
        issue_rows(idxn_ref, h2_hbm[0], f * lat_step + rb * lat_it, lat_it, hst, hslot, rb * lat_it,
                   sem_h, True, True)

    @pl.when(f < nf - 1)
    def _plain_step():
        def body(rb, carry):
            prefetch_lat(rb)
            ffn_rows(pl.multiple_of(rb * rblk, rblk), rblk)
            return carry
        lax.fori_loop(0, nb, body, 0)
        if r_ctx:
            ffn_rows(r_lat, r_ctx)

    @pl.when(f == nf - 1)
    def _last_step():
        blocks = [(i * rblk, rblk, 0, gt_rows[0][(i * rblk) // cap]) for i in range(nb)]
        if r_ctx:
            blocks.append((r_lat, r_ctx, 1, gt_rows[1][0]))
        last = len(blocks) - 1

        def gather_x(i, inline):
            r0, nrows, src, _ = blocks[i]
            issue_rows(idx_ref, x_hbm[src], r0, nrows, xst, i % x_slots, 0, sem_x, True, inline)

        def wait_x(i):
            _, nrows, src, _ = blocks[i]
            wait_tiles(x_hbm[src], xst, i % x_slots, nrows, sem_x)

        def update_and_scatter(i, inline):
            r0, nrows, src, gt_row = blocks[i]
            slot = i % x_slots
            wait_x(i)
            lane = lax.broadcasted_iota(jnp.int32, (nrows, LANES), 1)
            logits = _dot(xe[cur, r0:r0 + nrows, :], wr_ref[...])
            logits = jnp.where(lane < N_EXPERTS, logits, NEG_INF)
            ex = jnp.exp(logits - jnp.max(logits, axis=-1, keepdims=True))
            gval = (jnp.sum(jnp.where(lane == e, ex, 0.0), axis=-1, keepdims=True)
                    / jnp.sum(ex, axis=-1, keepdims=True))
            for k in range(TILE_ROWS):
                cols = slice(k * LANES, (k + 1) * LANES)
                ix = staged(xst, slot, 0, nrows, k)
                xst[ix] = xst[ix] + gt2_ref[gt_row:gt_row + 1, cols] * (acc[r0:r0 + nrows, cols] * gval)
            issue_rows(idx_ref, x_hbm[src], r0, nrows, xst, slot, 0, sem_x, False, inline)

        for i in range(min(ahead, last + 1)):
            gather_x(i, False)
        for i, (r0, nrows, _, _) in enumerate(blocks):
            if i >= 1:
                update_and_scatter(i - 1, True)
            if i < nb:
                prefetch_lat(i)
            if i >= 2:
                wait_x(i - 2)
            if i + ahead <= last:
                gather_x(i + ahead, True)
            ffn_rows(r0, nrows)
        update_and_scatter(last, False)
        if last >= 1:
            wait_x(last - 1)
        wait_x(last)

        @pl.when(e == ne - 1)
        def _drain():
            wait_tiles(h2_hbm[0], hst, (nf - 1) % 2, lat_step, sem_h)


def _expert_choice(groups, layer, w_gate, w_up, w_down, wr_pad, gt2):
    idx_parts, h2s, xs, counts, caps = [], [], [], [], []
    for h2, aff, x, _ in groups:
        b, n = x.shape[0], _num_tokens(x.shape)
        cap = CAPACITY_FACTOR * n // N_EXPERTS
        idx = _select_call(aff, cap)
        flat = idx + (jnp.arange(b, dtype=jnp.int32) * n)[:, None, None]
        idx_parts.append(jnp.transpose(flat, (1, 0, 2)).reshape(N_EXPERTS, b * cap))
        counts.append(b * cap)
        caps.append(cap)
        h2s.append(h2.reshape(b * n * TILE_ROWS, LANES))
        xs.append(x.reshape(b * n * TILE_ROWS, LANES))
    idx_all = jnp.concatenate(idx_parts, axis=1)
    outs = _expert_call(idx_all, h2s, xs, counts, caps[0], tuple(tuple(g[3]) for g in groups), layer,
                        w_gate, w_up, w_down, wr_pad, gt2)
    return [o.reshape(g[2].shape) for o, g in zip(outs, groups)]


def _expert_call(idx_all, h2s, xs, counts, cap, gt_rows, layer, w_gate, w_up, w_down, wr_pad, gt2):
    d = D_MODEL
    ff = w_gate.shape[-1]
    tf = FF_TILE
    nf = ff // tf
    nsrc = len(h2s)
    r_lat = counts[0]
    r_ctx = counts[1] if nsrc > 1 else 0
    rows = r_lat + r_ctx
    rblk = min(ROW_BLOCK, cap)
    idx_all = idx_all.reshape(N_EXPERTS, 1, rows)
    stage_tiles = r_lat // nf + r_ctx
    any_spec = pl.BlockSpec(memory_space=pl.ANY)
    idx_spec = lambda shift: pl.BlockSpec(
        (1, 1, rows), lambda ei, fi: (jnp.minimum(ei + shift, N_EXPERTS - 1), 0, 0), memory_space=pltpu.SMEM)
    return pl.pallas_call(
        functools.partial(_expert_kernel, nsrc=nsrc, nf=nf, ne=N_EXPERTS, r_lat=r_lat, r_ctx=r_ctx,
                          cap=cap, rblk=rblk, gt_rows=gt_rows),
        grid=(N_EXPERTS, nf),
        in_specs=[idx_spec(0), idx_spec(1)]
        + [any_spec] * (2 * nsrc)
        + [pl.BlockSpec((1, 1, d, tf), lambda ei, fi: (layer, ei, 0, fi)),
           pl.BlockSpec((1, 1, d, tf), lambda ei, fi: (layer, ei, 0, fi)),
           pl.BlockSpec((1, 1, tf, d), lambda ei, fi: (layer, ei, fi, 0)),
           pl.BlockSpec((d, LANES), lambda ei, fi: (0, 0)),
           pl.BlockSpec(gt2.shape, lambda ei, fi: (0, 0))],
        out_specs=[any_spec] * nsrc,
        out_shape=[jax.ShapeDtypeStruct(xa.shape, F32) for xa in xs],
        scratch_shapes=[pltpu.VMEM((2, rows, d), BF16), pltpu.VMEM((rows, d), F32),
                        pltpu.VMEM((2, stage_tiles * TILE_ROWS, LANES), F32),
                        pltpu.VMEM((X_SLOTS, rblk * TILE_ROWS, LANES), F32),
                        pltpu.SemaphoreType.DMA((2,)), pltpu.SemaphoreType.DMA((X_SLOTS,))],
        input_output_aliases={2 + nsrc + s: s for s in range(nsrc)},
        compiler_params=_cparams(("arbitrary", "arbitrary")),
        name="expert_ffn",
    )(idx_all, idx_all, *h2s, *xs, w_gate, w_up, w_down, wr_pad, gt2)


def _final_norm_kernel(x_ref, g_ref, o_ref):
    x = _load_tokens(x_ref)
    ms = jnp.mean(x * x, axis=-1, keepdims=True)
    o_ref[0] = x * lax.rsqrt(ms + EPS) * g_ref[...]


def _final_norm_call(x, g):
    b, n = x.shape[0], _num_tokens(x.shape)
    d = D_MODEL
    tm = min(TOKEN_TILE, n)
    return pl.pallas_call(
        _final_norm_kernel,
        grid=(b, n // tm),
        in_specs=[_token_spec(tm, x.shape),
                  pl.BlockSpec((1, d), lambda bi, i: (0, 0))],
        out_specs=pl.BlockSpec((1, tm, d), lambda bi, i: (bi, i, 0)),
        out_shape=jax.ShapeDtypeStruct((b, n, d), F32),
        compiler_params=_cparams(("parallel", "arbitrary")),
        name="final_norm",
    )(x, g)


def _rope_tables(n):
    rows = n // GRID_W
    row = np.repeat(np.arange(rows, dtype=np.float32), GRID_W)
    col = np.tile(np.arange(GRID_W, dtype=np.float32), rows)
    pairs = HEAD_DIM // 4
    inv = (np.float32(ROPE_BASE) ** (-np.arange(pairs, dtype=np.float32) / pairs)).astype(np.float32)
    ang = np.concatenate([row[:, None] * inv, col[:, None] * inv], axis=-1).astype(np.float32)
    cos, sin = np.cos(ang), np.sin(ang)
    cos_t = np.concatenate([cos, cos, cos, cos], axis=-1)
    sin_t = np.concatenate([-sin, sin, -sin, sin], axis=-1)
    return jnp.asarray(cos_t, F32), jnp.asarray(sin_t, F32)


def _band_bias(l_ctx):
    qi = np.arange(BLOCK)[:, None]
    kj = np.arange(3 * BLOCK)[None, :]
    ok = np.abs(kj - BLOCK - qi) <= WINDOW
    bias = np.concatenate([np.zeros((BLOCK, l_ctx)), np.where(ok, 0.0, NEG_INF)], axis=1).astype(np.float32)
    return jnp.asarray(np.concatenate([bias, bias], axis=0))


def kernel(x, c, ctx, c_ctx, w_mod, b_mod, g_mix, g_ffn, w_in, attn_sink, w_spatial, b_spatial,
           w_branch_attn, w_branch_sgu, w_branch_fourier, w_out, w_router, w_gate, w_up, w_down, g_final):
    b, n, d = x.shape
    l_ctx = ctx.shape[1]
    depth = w_mod.shape[0]

    cvec = jnp.zeros((SUBLANES, d), F32).at[0:b].set(c).at[b].set(c_ctx)
    mod = _mod_call(cvec, w_mod, b_mod).reshape(depth, SUBLANES, N_MOD, d)
    ctx_row = b

    cos_t, sin_t = _rope_tables(n)
    bias = _band_bias(l_ctx)
    no_rope = jnp.zeros((l_ctx, LANES), F32)
    gsz = SGU_WIDTH // SGU_GROUPS
    bd = jnp.asarray(np.kron(np.eye(SGU_GROUPS), np.ones((gsz, gsz))) / gsz, BF16)

    for layer in range(depth):
        last = layer == depth - 1
        w_l = w_in[layer].astype(BF16)
        g1 = g_mix[layer].reshape(1, d)
        g2 = g_ffn[layer].reshape(1, d)
        wsp = jnp.transpose(w_spatial[layer], (1, 0, 2)).reshape(CHUNK, SGU_GROUPS * CHUNK).astype(BF16)
        bsp = jnp.repeat(jnp.transpose(b_spatial[layer]), gsz, axis=1)
        wba = w_branch_attn[layer].astype(BF16)
        wbs = w_branch_sgu[layer].astype(BF16)
        wbf = w_branch_fourier[layer].astype(BF16)
        wo = w_out[layer].astype(BF16)
        wr_t = jnp.transpose(w_router[layer]).astype(BF16)
        wr_pad = jnp.pad(w_router[layer], ((0, 0), (0, LANES - N_EXPERTS))).astype(BF16)
        sink = attn_sink[layer]
        mod_l = mod[layer]

        qc, kcd, vcd, uc, vsc, fc, gc = _proj_call(ctx, mod_l, ctx_row, g1, w_l, no_rope, no_rope, rope=False)
        if not last:
            a_c = _attn_call(qc, None, None, kcd, vcd, sink, None, local=False)
            f_c = _fourier_call(fc)
            ctx_mid, h2c, aff_c = _merge_call(a_c, uc, vsc, f_c, gc, ctx, mod_l, ctx_row, g2,
                                              wsp, bsp, bd, wba, wbs, wbf, wo, wr_t)

        q, kd, vd, u, vs, f, gates = _proj_call(x, mod_l, None, g1, w_l, cos_t, sin_t, rope=True)
        a = _attn_call(q, kd, vd, kcd, vcd, sink, bias, local=True)
        fo = _fourier_call(f)
        x_mid, h2, aff = _merge_call(a, u, vs, fo, gates, x, mod_l, None, g2,
                                     wsp, bsp, bd, wba, wbs, wbf, wo, wr_t)

        gt2 = mod_l[:, 5]
        groups = [(h2, aff, x_mid, list(range(b)))]
        if not last:
            groups.append((h2c, aff_c, ctx_mid, [ctx_row] * b))
        outs = _expert_choice(groups, layer, w_gate, w_up, w_down, wr_pad, gt2)
        x = outs[0]
        if not last:
            ctx = outs[1]
    return _final_norm_call(x, g_final.reshape(1, d))
```

```python
import functools
import math

import numpy as np
import jax
import jax.numpy as jnp
from jax import lax
from jax.experimental import pallas as pl
from jax.experimental.pallas import tpu as pltpu

F32 = jnp.float32
BF16 = jnp.bfloat16

D_MODEL = 1024
GRID_W = 64
HEAD_DIM = 64
N_HEADS = 8
N_KV_HEADS = 2
ATTN_WIDTH = N_HEADS * HEAD_DIM
KV_WIDTH = N_KV_HEADS * HEAD_DIM
WINDOW = 128
BLOCK = 128
ROPE_BASE = 10000.0
SGU_GROUPS = 4
SGU_WIDTH = 256
CHUNK = 128
FNET_GROUPS = 4
FNET_GROUP_DIM = 64
FNET_WIDTH = 256
N_EXPERTS = 16
CAPACITY_FACTOR = 2
EXPERT_FF = 2048
N_MOD = 6
EPS = 1e-6
NEG_INF = -1e30
OFF_KV = ATTN_WIDTH
OFF_U = ATTN_WIDTH + 2 * KV_WIDTH
OFF_G = OFF_U + 2 * SGU_WIDTH + FNET_WIDTH
IN_WIDTH = OFF_G + 3 * D_MODEL

LANES = 128
SUBLANES = 8
VMEM_LIMIT_BYTES = 56 * 1024 * 1024

TOKEN_TILE = 512
ATTN_TILE = 256
SEL_ROWS = 128
FF_TILE = 256
ROW_BLOCK = 512
PLAIN_ROW_BLOCK = 1024
DMA_UNROLL = 8
X_SLOTS = 3


def _cparams(sem):
    return pltpu.CompilerParams(dimension_semantics=sem, vmem_limit_bytes=VMEM_LIMIT_BYTES)


def _dot(a, b):
    return jnp.dot(a, b, preferred_element_type=F32)


def _dot_t(a, b):
    return lax.dot_general(a, b, (((1,), (1,)), ((), ())), preferred_element_type=F32)


def _mod_kernel(c_ref, w_ref, b_ref, o_ref):
    c = c_ref[...]
    s = (c * jax.nn.sigmoid(c)).astype(BF16)
    o_ref[0] = _dot(s, w_ref[0].astype(BF16)) + b_ref[0]


def _mod_call(cvec, w_mod, b_mod):
    depth, d, width = w_mod.shape
    tn = 1536
    return pl.pallas_call(
        _mod_kernel,
        grid=(depth, width // tn),
        in_specs=[
            pl.BlockSpec((SUBLANES, d), lambda l, j: (0, 0)),
            pl.BlockSpec((1, d, tn), lambda l, j: (l, 0, j)),
            pl.BlockSpec((1, 1, tn), lambda l, j: (l, 0, j)),
        ],
        out_specs=pl.BlockSpec((1, SUBLANES, tn), lambda l, j: (l, 0, j)),
        out_shape=jax.ShapeDtypeStruct((depth, SUBLANES, width), F32),
        compiler_params=_cparams(("arbitrary", "arbitrary")),
        name="adaln_mod",
    )(cvec, w_mod, b_mod.reshape(depth, 1, width))


def _swap_halves(t):
    lane = lax.broadcasted_iota(jnp.int32, t.shape, 1)
    first = (lane % HEAD_DIM) < (HEAD_DIM // 2)
    return jnp.where(first, pltpu.roll(t, LANES - HEAD_DIM // 2, 1), pltpu.roll(t, HEAD_DIM // 2, 1))


def _dup_heads(t):
    lane = lax.broadcasted_iota(jnp.int32, t.shape, 1)
    lo = lane < HEAD_DIM
    r = pltpu.roll(t, HEAD_DIM, 1)
    return jnp.concatenate([jnp.where(lo, t, r), jnp.where(lo, r, t)], axis=1)


TILE_ROWS = D_MODEL // LANES


def _is_row_layout(shape):
    return shape[-1] == D_MODEL


def _num_tokens(shape):
    return shape[1] if _is_row_layout(shape) else shape[1] // TILE_ROWS


def _load_tokens(ref):
    if _is_row_layout(ref.shape):
        return ref[0]
    tm = ref.shape[1] // TILE_ROWS
    return jnp.concatenate([ref[0, pl.ds(k, tm, stride=TILE_ROWS), :] for k in range(TILE_ROWS)], axis=1)


def _store_tokens(ref, val):
    if _is_row_layout(ref.shape):
        ref[0] = val
    else:
        tm = ref.shape[1] // TILE_ROWS
        for k in range(TILE_ROWS):
            ref[0, pl.ds(k, tm, stride=TILE_ROWS), :] = val[:, k * LANES:(k + 1) * LANES]


def _token_spec(tm, shape):
    rows = tm if _is_row_layout(shape) else tm * TILE_ROWS
    return pl.BlockSpec((1, rows, shape[2]), lambda bi, i: (bi, i, 0))


def _proj_kernel(x_ref, m_ref, g_ref, w_ref, cos_ref, sin_ref,
                 q_ref, k_ref, v_ref, u_ref, vs_ref, f_ref, gt_ref, *, rope):
    x = _load_tokens(x_ref)
    ms = jnp.mean(x * x, axis=-1, keepdims=True)
    y = x * lax.rsqrt(ms + EPS) * g_ref[...]
    h = y * (1.0 + m_ref[0, 1:2, :]) + m_ref[0, 0:1, :]
    hb = h.astype(BF16)

    if rope:
        cos = cos_ref[...]
        sin = sin_ref[...]

    def rot(t):
        if not rope:
            return t
        return t * cos + _swap_halves(t) * sin

    q = _dot(hb, w_ref[:, 0:OFF_KV])
    for j in range(ATTN_WIDTH // LANES):
        qj = rot(q[:, j * LANES:(j + 1) * LANES]) * (HEAD_DIM ** -0.5)
        q_ref[0, :, j * LANES:(j + 1) * LANES] = qj.astype(BF16)
    kv = _dot(hb, w_ref[:, OFF_KV:OFF_U])
    k_ref[0] = _dup_heads(rot(kv[:, 0:KV_WIDTH])).astype(BF16)
    v_ref[0] = _dup_heads(kv[:, KV_WIDTH:2 * KV_WIDTH]).astype(BF16)
    uvf = _dot(hb, w_ref[:, OFF_U:OFF_G])
    u_ref[0] = uvf[:, 0:SGU_WIDTH].astype(BF16)
    vs_ref[0] = uvf[:, SGU_WIDTH:2 * SGU_WIDTH].astype(BF16)
    f_ref[0] = uvf[:, 2 * SGU_WIDTH:].astype(BF16)
    for j in range(3):
        gj = _dot(hb, w_ref[:, OFF_G + j * D_MODEL:OFF_G + (j + 1) * D_MODEL])
        gt_ref[0, :, j * D_MODEL:(j + 1) * D_MODEL] = gj.astype(BF16)


def _proj_call(x, mod, mod_row, g, w, cos_t, sin_t, *, rope):
    b, n = x.shape[0], _num_tokens(x.shape)
    d = D_MODEL
    tm = min(TOKEN_TILE, n)
    if mod_row is None:
        mod_map = lambda bi, i: (bi, 0, 0)
    else:
        mod_map = lambda bi, i: (mod_row, 0, 0)
    widths = [ATTN_WIDTH, 2 * KV_WIDTH, 2 * KV_WIDTH, SGU_WIDTH, SGU_WIDTH, FNET_WIDTH, 3 * D_MODEL]
    return pl.pallas_call(
        functools.partial(_proj_kernel, rope=rope),
        grid=(b, n // tm),
        in_specs=[
            _token_spec(tm, x.shape),
            pl.BlockSpec((1, N_MOD, d), mod_map),
            pl.BlockSpec((1, d), lambda bi, i: (0, 0)),
            pl.BlockSpec((d, IN_WIDTH), lambda bi, i: (0, 0)),
            pl.BlockSpec((tm, LANES), lambda bi, i: (i, 0)),
            pl.BlockSpec((tm, LANES), lambda bi, i: (i, 0)),
        ],
        out_specs=[pl.BlockSpec((1, tm, wd), lambda bi, i: (bi, i, 0)) for wd in widths],
        out_shape=[jax.ShapeDtypeStruct((b, n, wd), BF16) for wd in widths],
        compiler_params=_cparams(("parallel", "arbitrary")),
        name="norm_in_proj",
    )(x, mod, g, w, cos_t, sin_t)


def _attn_kernel(sink_ref, q_ref, *rest, nsub, local):
    if local:
        kp_ref, ko_ref, kn_ref, vp_ref, vo_ref, vn_ref, kc_ref, vc_ref, bias_ref, o_ref = rest
        l_ctx = kc_ref.shape[1]
        i = pl.program_id(1)
        last = pl.num_programs(1) - 1
        kcat = jnp.concatenate([kp_ref[0], ko_ref[0], kn_ref[0]], axis=0)
        vcat = jnp.concatenate([vp_ref[0], vo_ref[0], vn_ref[0]], axis=0)
        bias = bias_ref[...]
        col = lax.broadcasted_iota(jnp.int32, (1, l_ctx + 3 * BLOCK), 1)
        lo_cols = jnp.logical_and(col >= l_ctx, col < l_ctx + BLOCK).astype(F32)
        hi_cols = (col >= l_ctx + 2 * BLOCK).astype(F32)
        edge_lo = jnp.where(i == 0, NEG_INF, 0.0)
        edge_hi = jnp.where(i == last, NEG_INF, 0.0)
    else:
        kc_ref, vc_ref, o_ref = rest
    kc = kc_ref[0]
    vc = vc_ref[0]
    lane_k = lax.broadcasted_iota(jnp.int32, (1, LANES), 1)
    even_k = lane_k < HEAD_DIM
    zero = jnp.zeros((), BF16)
    lane_o = lax.broadcasted_iota(jnp.int32, (BLOCK, LANES), 1)

    for j in range(nsub):
        r0 = j * BLOCK
        pairs = []
        for g in range(N_KV_HEADS):
            c0 = g * 2 * LANES
            lhs = jnp.concatenate([q_ref[0, r0:r0 + BLOCK, c0:c0 + LANES],
                                   q_ref[0, r0:r0 + BLOCK, c0 + LANES:c0 + 2 * LANES]], axis=0)
            keys = kc[:, g * LANES:(g + 1) * LANES]
            vals = vc[:, g * LANES:(g + 1) * LANES]
            if local:
                keys = jnp.concatenate([keys, kcat[r0:r0 + 3 * BLOCK, g * LANES:(g + 1) * LANES]], axis=0)
                vals = jnp.concatenate([vals, vcat[r0:r0 + 3 * BLOCK, g * LANES:(g + 1) * LANES]], axis=0)
                bj = bias
                if j == 0:
                    bj = bj + lo_cols * edge_lo
                if j == nsub - 1:
                    bj = bj + hi_cols * edge_hi
            halves = []
            for par in range(2):
                sel = even_k if par == 0 else jnp.logical_not(even_k)
                s = _dot_t(lhs, jnp.where(sel, keys, zero))
                if local:
                    s = s + bj
                sk = jnp.concatenate([jnp.full((BLOCK, 1), sink_ref[4 * g + par], F32),
                                      jnp.full((BLOCK, 1), sink_ref[4 * g + 2 + par], F32)], axis=0)
                m = jnp.maximum(jnp.max(s, axis=-1, keepdims=True), sk)
                p = jnp.exp(s - m)
                den = jnp.sum(p, axis=-1, keepdims=True) + jnp.exp(sk - m)
                halves.append(_dot(p.astype(BF16), vals) / den)
            for pr in range(2):
                pairs.append(jnp.where(lane_o < HEAD_DIM,
                                       halves[0][pr * BLOCK:(pr + 1) * BLOCK],
                                       halves[1][pr * BLOCK:(pr + 1) * BLOCK]))
        o_ref[0, r0:r0 + BLOCK, :] = jnp.concatenate(pairs, axis=1).astype(BF16)


def _attn_call(q, kd, vd, kcd, vcd, sink, bias, *, local):
    b, n, _ = q.shape
    l_ctx = kcd.shape[1]
    tq = min(ATTN_TILE, n)
    nsub = tq // BLOCK
    nblk = n // BLOCK
    kw = 2 * KV_WIDTH
    in_specs = [pl.BlockSpec(memory_space=pltpu.SMEM),
                pl.BlockSpec((1, tq, ATTN_WIDTH), lambda bi, i: (bi, i, 0))]
    args = [sink, q]
    if local:
        prev_map = lambda bi, i: (bi, jnp.maximum(i * nsub - 1, 0), 0)
        own_map = lambda bi, i: (bi, i, 0)
        next_map = lambda bi, i: (bi, jnp.minimum((i + 1) * nsub, nblk - 1), 0)
        for arr in (kd, vd):
            in_specs += [pl.BlockSpec((1, BLOCK, kw), prev_map),
                         pl.BlockSpec((1, tq, kw), own_map),
                         pl.BlockSpec((1, BLOCK, kw), next_map)]
            args += [arr, arr, arr]
    in_specs += [pl.BlockSpec((1, l_ctx, kw), lambda bi, i: (bi, 0, 0)),
                 pl.BlockSpec((1, l_ctx, kw), lambda bi, i: (bi, 0, 0))]
    args += [kcd, vcd]
    if local:
        in_specs.append(pl.BlockSpec((2 * BLOCK, l_ctx + 3 * BLOCK), lambda bi, i: (0, 0)))
        args.append(bias)
    return pl.pallas_call(
        functools.partial(_attn_kernel, nsub=nsub, local=local),
        grid=(b, n // tq),
        in_specs=in_specs,
        out_specs=pl.BlockSpec((1, tq, ATTN_WIDTH), lambda bi, i: (bi, i, 0)),
        out_shape=jax.ShapeDtypeStruct((b, n, ATTN_WIDTH), BF16),
        compiler_params=_cparams(("parallel", "arbitrary")),
        name="window_attention" if local else "context_attention",
    )(*args)


def _fft1_kernel(w_ref, x_ref, z_ref):
    z_ref[0] = _dot(w_ref[...].astype(BF16), x_ref[0]).astype(BF16)


def _fft2_kernel(t_ref, z_ref, cs_ref, o_ref, *, kb):
    n2 = z_ref.shape[3]
    cs = cs_ref[...].astype(BF16)
    for j in range(kb):
        zk = jnp.concatenate([z_ref[0, 0, j], z_ref[0, 1, j]], axis=0)
        y = _dot(t_ref[j].astype(BF16), zk)
        yc = jnp.concatenate([y[0:n2], y[n2:2 * n2]], axis=1).astype(BF16)
        o_ref[0, :, j, :] = _dot(yc, cs).astype(BF16)


def _fft_dense_kernel(cn_ref, sn_ref, cc_ref, sc_ref, x_ref, o_ref):
    x = x_ref[0]
    a = _dot(x, cc_ref[...].astype(BF16)).astype(BF16)
    bm = _dot(x, sc_ref[...].astype(BF16)).astype(BF16)
    o_ref[0] = (_dot(cn_ref[...].astype(BF16), a) - _dot(sn_ref[...].astype(BF16), bm)).astype(BF16)


def _channel_dft():
    c = np.arange(FNET_GROUP_DIM)
    ang = 2.0 * np.pi * np.outer(c, c) / FNET_GROUP_DIM
    eye = np.eye(FNET_GROUPS)
    cc = np.kron(eye, np.cos(ang)) / math.sqrt(FNET_GROUP_DIM)
    sc = np.kron(eye, np.sin(ang)) / math.sqrt(FNET_GROUP_DIM)
    return cc, sc


@functools.lru_cache(maxsize=None)
def _fft_tables(n):
    n2 = LANES
    n1 = n // n2
    t1 = np.arange(n1)
    a1 = 2.0 * np.pi * np.outer(t1, t1) / n1
    w1 = np.concatenate([np.cos(a1), -np.sin(a1)], axis=0) / math.sqrt(n)
    k = np.arange(n1)[:, None, None] + n1 * np.arange(n2)[None, :, None]
    a2 = 2.0 * np.pi * k * np.arange(n2)[None, None, :] / n
    mr, mi = np.cos(a2), -np.sin(a2)
    tb = np.concatenate([np.concatenate([mr, -mi], axis=2),
                         np.concatenate([mi, mr], axis=2)], axis=1)
    cc, sc = _channel_dft()
    cs = np.concatenate([cc, sc], axis=0)
    return (jnp.asarray(w1, F32), jnp.asarray(tb, F32), jnp.asarray(cs, F32))


@functools.lru_cache(maxsize=None)
def _fft_dense_tables(n):
    t = np.arange(n)
    a = 2.0 * np.pi * np.outer(t, t) / n
    cc, sc = _channel_dft()
    return (jnp.asarray(np.cos(a) / math.sqrt(n), F32), jnp.asarray(np.sin(a) / math.sqrt(n), F32),
            jnp.asarray(cc, F32), jnp.asarray(sc, F32))


def _fourier_call(f):
    b, n, c = f.shape
    if n <= 2 * LANES:
        cn, sn, cc, sc = _fft_dense_tables(n)
        full = lambda bi: (0, 0)
        return pl.pallas_call(
            _fft_dense_kernel,
            grid=(b,),
            in_specs=[pl.BlockSpec((n, n), full), pl.BlockSpec((n, n), full),
                      pl.BlockSpec((c, c), full), pl.BlockSpec((c, c), full),
                      pl.BlockSpec((1, n, c), lambda bi: (bi, 0, 0))],
            out_specs=pl.BlockSpec((1, n, c), lambda bi: (bi, 0, 0)),
            out_shape=jax.ShapeDtypeStruct((b, n, c), BF16),
            compiler_params=_cparams(("parallel",)),
            name="fourier_dense",
        )(cn, sn, cc, sc, f)
    n2 = LANES
    n1 = n // n2
    w1, tb, cs = _fft_tables(n)
    cols = n2 * c
    tc = 2048
    z = pl.pallas_call(
        _fft1_kernel,
        grid=(b, cols // tc),
        in_specs=[pl.BlockSpec((2 * n1, n1), lambda bi, j: (0, 0)),
                  pl.BlockSpec((1, n1, tc), lambda bi, j: (bi, 0, j))],
        out_specs=pl.BlockSpec((1, 2 * n1, tc), lambda bi, j: (bi, 0, j)),
        out_shape=jax.ShapeDtypeStruct((b, 2 * n1, cols), BF16),
        compiler_params=_cparams(("parallel", "arbitrary")),
        name="fourier_stage1",
    )(w1, f.reshape(b, n1, cols))
    kb = SUBLANES
    out = pl.pallas_call(
        functools.partial(_fft2_kernel, kb=kb),
        grid=(n1 // kb, b),
        in_specs=[pl.BlockSpec((kb, 2 * n2, 2 * n2), lambda j, bi: (j, 0, 0)),
                  pl.BlockSpec((1, 2, kb, n2, c), lambda j, bi: (bi, 0, j, 0, 0)),
                  pl.BlockSpec((2 * c, c), lambda j, bi: (0, 0))],
        out_specs=pl.BlockSpec((1, n2, kb, c), lambda j, bi: (bi, 0, j, 0)),
        out_shape=jax.ShapeDtypeStruct((b, n2, n1, c), BF16),
        compiler_params=_cparams(("arbitrary", "arbitrary")),
        name="fourier_stage2",
    )(tb, z.reshape(b, 2, n1, n2, c), cs)
    return out.reshape(b, n, c)


def _merge_kernel(a_ref, u_ref, vs_ref, fo_ref, gt_ref, x_ref, m_ref, g2_ref,
                  wsp_ref, bsp_ref, bd_ref, wba_ref, wbs_ref, wbf_ref, wo_ref, wr_ref,
                  xo_ref, h2_ref, aff_ref, s_scr):
    tm = a_ref.shape[1]
    lane = lax.broadcasted_iota(jnp.int32, (CHUNK, SGU_WIDTH), 1)
    for c in range(tm // CHUNK):
        r0 = c * CHUNK
        ug = jax.nn.gelu(u_ref[0, r0:r0 + CHUNK, :].astype(F32))
        vg = jax.nn.gelu(vs_ref[0, r0:r0 + CHUNK, :].astype(F32))
        sq = vg * vg
        sq_hi = sq.astype(BF16)
        sq_lo = (sq - sq_hi.astype(F32)).astype(BF16)
        msq = _dot(sq_hi, bd_ref[...]) + _dot(sq_lo, bd_ref[...])
        vn = (vg * lax.rsqrt(msq + EPS)).astype(BF16)
        zero = jnp.zeros((), BF16)
        stack = jnp.concatenate(
            [jnp.where(lax.shift_right_logical(lane, 6) == g, vn, zero) for g in range(SGU_GROUPS)], axis=0)
        z = _dot(wsp_ref[...], stack) + bsp_ref[...]
        s_scr[r0:r0 + CHUNK, :] = (ug * z).astype(BF16)

    gates = gt_ref[0]
    acc = jax.nn.sigmoid(gates[:, 0:D_MODEL].astype(F32)) * _dot(a_ref[0], wba_ref[...])
    acc = acc + jax.nn.sigmoid(gates[:, D_MODEL:2 * D_MODEL].astype(F32)) * _dot(s_scr[...], wbs_ref[...])
    acc = acc + jax.nn.sigmoid(gates[:, 2 * D_MODEL:].astype(F32)) * _dot(fo_ref[0], wbf_ref[...])
    o = _dot(acc.astype(BF16), wo_ref[...])
    xn = _load_tokens(x_ref) + m_ref[0, 2:3, :] * o
    _store_tokens(xo_ref, xn)
    ms = jnp.mean(xn * xn, axis=-1, keepdims=True)
    y = xn * lax.rsqrt(ms + EPS) * g2_ref[...]
    h2 = y * (1.0 + m_ref[0, 4:5, :]) + m_ref[0, 3:4, :]
    _store_tokens(h2_ref, h2)
    logits = _dot_t(wr_ref[...], h2.astype(BF16))
    mx = jnp.max(logits, axis=0, keepdims=True)
    ex = jnp.exp(logits - mx)
    aff_ref[0] = ex / jnp.sum(ex, axis=0, keepdims=True)


def _merge_call(a, u, vs, fo, gates, x, mod, mod_row, g2, wsp, bsp, bd, wba, wbs, wbf, wo, wr_t):
    b, n = x.shape[0], _num_tokens(x.shape)
    d = D_MODEL
    tm = min(TOKEN_TILE, n)
    if mod_row is None:
        mod_map = lambda bi, i: (bi, 0, 0)
    else:
        mod_map = lambda bi, i: (mod_row, 0, 0)
    tok = lambda wd: pl.BlockSpec((1, tm, wd), lambda bi, i: (bi, i, 0))
    full = lambda arr: pl.BlockSpec(arr.shape, lambda bi, i: (0,) * arr.ndim)
    tiled = (b, n * TILE_ROWS, LANES)
    return pl.pallas_call(
        _merge_kernel,
        grid=(b, n // tm),
        in_specs=[tok(ATTN_WIDTH), tok(SGU_WIDTH), tok(SGU_WIDTH), tok(FNET_WIDTH), tok(3 * D_MODEL),
                  _token_spec(tm, x.shape),
                  pl.BlockSpec((1, N_MOD, d), mod_map), full(g2),
                  full(wsp), full(bsp), full(bd), full(wba), full(wbs), full(wbf), full(wo), full(wr_t)],
        out_specs=[_token_spec(tm, tiled), _token_spec(tm, tiled),
                   pl.BlockSpec((1, N_EXPERTS, tm), lambda bi, i: (bi, 0, i))],
        out_shape=[jax.ShapeDtypeStruct(tiled, F32), jax.ShapeDtypeStruct(tiled, F32),
                   jax.ShapeDtypeStruct((b, N_EXPERTS, n), F32)],
        scratch_shapes=[pltpu.VMEM((tm, SGU_WIDTH), BF16)],
        compiler_params=_cparams(("parallel", "arbitrary")),
        name="merge_norm_router",
    )(a, u, vs, fo, gates, x, mod, g2, wsp, bsp, bd, wba, wbs, wbf, wo, wr_t)


def _select_kernel(aff_ref, tri_ref, lstrict_ref, idx_ref, *, cap, slots):
    rows = SEL_ROWS
    ones = jnp.ones((LANES, LANES), BF16)
    tri = tri_ref[...]
    lstrict = lstrict_ref[...]
    lane_r = lax.broadcasted_iota(jnp.int32, (rows, LANES), 1)
    row_r = lax.broadcasted_iota(jnp.int32, (rows, LANES), 0)
    eye = lane_r == row_r

    def total(mask_i32):
        return jnp.sum(jnp.sum(mask_i32, axis=0, keepdims=True), axis=1, keepdims=True)

    def cumsum(mask):
        mb = jnp.where(mask, 1.0, 0.0).astype(BF16)
        within = _dot(mb, tri)
        tot = jnp.broadcast_to(within[:, LANES - 1:LANES], (rows, LANES)).astype(BF16)
        return within + _dot(lstrict, tot)

    def split(v):
        hi = jnp.floor(v * (1.0 / 256.0))
        return hi.astype(BF16), (v - hi * 256.0).astype(BF16)

    aff_all = [aff_ref[0, e] for e in range(N_EXPERTS)]

    def as_f32(bits):
        return lax.bitcast_convert_type(bits, F32)

    def bit_step(t, prefixes):
        bit = lax.shift_left(jnp.int32(1), 30 - t)
        out = []
        for e in range(N_EXPERTS):
            cand = prefixes[e] | bit
            cnt = total((aff_all[e] >= as_f32(cand)).astype(jnp.int32))
            out.append(jnp.where(cnt >= cap, cand, prefixes[e]))
        return tuple(out)

    thr = lax.fori_loop(0, 31, bit_step, tuple(jnp.zeros((1, 1), jnp.int32) for _ in range(N_EXPERTS)))

    slot = lax.broadcasted_iota(jnp.int32, (slots, LANES), 0).astype(F32)
    lane_s = lax.broadcasted_iota(jnp.int32, (slots, LANES), 1).astype(F32)
    for e in range(N_EXPERTS):
        gt = aff_all[e] >= as_f32(thr[e] + 1)
        eq = jnp.logical_and(aff_all[e] >= as_f32(thr[e]), jnp.logical_not(gt))
        need = (cap - total(gt.astype(jnp.int32))).astype(F32)
        eq_rank = cumsum(eq) - jnp.where(eq, 1.0, 0.0)
        sel = jnp.logical_or(gt, jnp.logical_and(eq, eq_rank < need))
        gcum = cumsum(sel)
        ends = jnp.where(eye, jnp.broadcast_to(gcum[:, LANES - 1:LANES], (rows, LANES)), 0.0)
        e_hi, e_lo = split(ends)
        ones_s = jnp.ones((slots, rows), BF16)
        ends_row = _dot(ones_s, e_hi) * 256.0 + _dot(ones_s, e_lo)
        row_of = _dot(jnp.where(ends_row <= slot, 1.0, 0.0).astype(BF16), ones)
        onehot = jnp.where(lane_s == row_of, 1.0, 0.0).astype(BF16)
        g_hi, g_lo = split(gcum)
        grow = _dot(onehot, g_hi) * 256.0 + _dot(onehot, g_lo)
        lane_of = _dot(jnp.where(grow <= slot, 1.0, 0.0).astype(BF16), ones)
        token = (row_of * float(LANES) + lane_of).astype(jnp.int32)
        for blk in range(slots // LANES):
            tb = token[blk * LANES:(blk + 1) * LANES, :]
            idx_ref[0, e, blk:blk + 1, :] = jnp.sum(jnp.where(eye, tb, 0), axis=0, keepdims=True)


def _select_call(aff, cap):
    b, e, n = aff.shape
    full = SEL_ROWS * LANES
    if n < full:
        aff = jnp.pad(aff, ((0, 0), (0, 0), (0, full - n)), constant_values=-1.0)
    slots = max(LANES, cap)
    r = np.arange(LANES)
    tri = jnp.asarray(r[:, None] <= r[None, :], BF16)
    lstrict = jnp.asarray(r[None, :] < r[:, None], BF16)
    idx = pl.pallas_call(
        functools.partial(_select_kernel, cap=cap, slots=slots),
        grid=(b,),
        in_specs=[pl.BlockSpec((1, e, SEL_ROWS, LANES), lambda bi: (bi, 0, 0, 0)),
                  pl.BlockSpec((LANES, LANES), lambda bi: (0, 0)),
                  pl.BlockSpec((LANES, LANES), lambda bi: (0, 0))],
        out_specs=pl.BlockSpec((1, e, slots // LANES, LANES), lambda bi: (bi, 0, 0, 0)),
        out_shape=jax.ShapeDtypeStruct((b, e, slots // LANES, LANES), jnp.int32),
        compiler_params=_cparams(("parallel",)),
        name="expert_choice_select",
    )(aff.reshape(b, e, SEL_ROWS, LANES), tri, lstrict)
    return idx.reshape(b, e, slots)[:, :, :cap]


def _expert_kernel(idx_ref, idxn_ref, *refs, nsrc, nf, ne, r_lat, r_ctx, cap, rblk, gt_rows):
    h2_hbm = refs[0:nsrc]
    wg_ref, wu_ref, wd_ref, wr_ref, gt2_ref = refs[2 * nsrc:2 * nsrc + 5]
    x_hbm = refs[2 * nsrc + 5:3 * nsrc + 5]
    xe, acc, hst, xst, sem_h, sem_x = refs[3 * nsrc + 5:]
    e = pl.program_id(0)
    f = pl.program_id(1)
    cur = lax.rem(e, 2)
    nxt = 1 - cur
    nb = r_lat // rblk
    lat_step = r_lat // nf
    lat_it = lat_step // nb
    assert lat_step * nf == r_lat and lat_it * nb == lat_step and lat_step % DMA_UNROLL == 0
    assert r_ctx % DMA_UNROLL == 0 and rblk % DMA_UNROLL == 0 and cap % rblk == 0
    x_slots = xst.shape[0]

    def tile_dma(hbm, tok, buf, slot, pos, sem, to_vmem):
        h_tile = hbm.at[pl.ds(pl.multiple_of(tok * TILE_ROWS, TILE_ROWS), TILE_ROWS), :]
        v_tile = buf.at[slot, pl.ds(pl.multiple_of(pos * TILE_ROWS, TILE_ROWS), TILE_ROWS), :]
        if to_vmem:
            pltpu.make_async_copy(h_tile, v_tile, sem.at[slot]).start()
        else:
            pltpu.make_async_copy(v_tile, h_tile, sem.at[slot]).start()

    def issue_rows(iref, hbm, first, count, buf, slot, pos0, sem, to_vmem, inline):
        if inline:
            for s in range(count):
                tile_dma(hbm, iref[0, 0, first + s], buf, slot, pos0 + s, sem, to_vmem)
            return

        def body(t, carry):
            for s in range(DMA_UNROLL):
                r = t * DMA_UNROLL + s
                tile_dma(hbm, iref[0, 0, first + r], buf, slot, pos0 + r, sem, to_vmem)
            return carry
        lax.fori_loop(0, count // DMA_UNROLL, body, 0)

    def wait_tiles(hbm, buf, slot, ntiles, sem):
        pltpu.make_async_copy(hbm.at[pl.ds(0, ntiles * TILE_ROWS), :],
                              buf.at[slot, pl.ds(0, ntiles * TILE_ROWS), :], sem.at[slot]).wait()

    def staged(buf, slot, first_tile, ntiles, k):
        return (slot, pl.ds(first_tile * TILE_ROWS + k, ntiles, stride=TILE_ROWS), slice(None))

    def finish_chunk(c, dst):
        slot = c % 2
        wait_tiles(h2_hbm[0], hst, slot, lat_step, sem_h)
        if c == 0 and r_ctx:
            wait_tiles(h2_hbm[1], hst, 0, r_ctx, sem_h)
        for k in range(TILE_ROWS):
            cols = slice(k * LANES, (k + 1) * LANES)
            xe[dst, c * lat_step:(c + 1) * lat_step, cols] = hst[staged(hst, slot, 0, lat_step, k)].astype(BF16)
            if c == 0 and r_ctx:
                xe[dst, r_lat:r_lat + r_ctx, cols] = hst[staged(hst, 0, lat_step, r_ctx, k)].astype(BF16)

    for j in range(nf):
        @pl.when(f == j)
        def _land(j=j):
            if j == 0:
                @pl.when(e == 0)
                def _first_expert():
                    for c in range(nf):
                        issue_rows(idx_ref, h2_hbm[0], c * lat_step, lat_step, hst, c % 2, 0, sem_h, True, False)
                        if c == 0 and r_ctx:
                            issue_rows(idx_ref, h2_hbm[1], r_lat, r_ctx, hst, 0, lat_step, sem_h, True, False)
                        finish_chunk(c, cur)

                @pl.when(e > 0)
                def _last_chunk():
                    finish_chunk(nf - 1, cur)
                acc[...] = jnp.zeros_like(acc)
                if r_ctx:
                    issue_rows(idxn_ref, h2_hbm[1], r_lat, r_ctx, hst, 0, lat_step, sem_h, True, False)
            else:
                finish_chunk(j - 1, nxt)

    wg = wg_ref[0, 0].astype(BF16)
    wu = wu_ref[0, 0].astype(BF16)
    wd = wd_ref[0, 0].astype(BF16)
    hslot = lax.rem(f, 2)

    def ffn_rows(r0, nrows):
        xs = xe[cur, pl.ds(r0, nrows), :]
        a = _dot(xs, wg)
        u = _dot(xs, wu)
        hm = (a * jax.nn.sigmoid(a) * u).astype(BF16)
        acc[pl.ds(r0, nrows), :] += _dot(hm, wd)

    def prefetch_lat(rb, per_block=lat_it):
        issue_rows(idxn_ref, h2_hbm[0], f * lat_step + rb * per_block, per_block, hst, hslot, rb * per_block,
                   sem_h, True, True)

    @pl.when(f < nf - 1)
    def _plain_step():
        big = min(PLAIN_ROW_BLOCK, r_lat)
        nbig = r_lat // big
        assert nbig * big == r_lat and lat_step % nbig == 0

        def body(rb, carry):
            prefetch_lat(rb, lat_step // nbig)
            ffn_rows(pl.multiple_of(rb * big, big), big)
            return carry
        lax.fori_loop(0, nbig, body, 0)
        if r_ctx:
            ffn_rows(r_lat, r_ctx)

    @pl.when(f == nf - 1)
    def _last_step():
        blocks = [(i * rblk, rblk, 0, gt_rows[0][(i * rblk) // cap]) for i in range(nb)]
        if r_ctx:
            blocks.append((r_lat, r_ctx, 1, gt_rows[1][0]))
        last = len(blocks) - 1

        def gather_x(i, inline):
            r0, nrows, src, _ = blocks[i]
            issue_rows(idx_ref, x_hbm[src], r0, nrows, xst, i % x_slots, 0, sem_x, True, inline)

        def wait_x(i):
            _, nrows, src, _ = blocks[i]
            wait_tiles(x_hbm[src], xst, i % x_slots, nrows, sem_x)

        def update_and_scatter(i, inline):
            r0, nrows, src, gt_row = blocks[i]
            slot = i % x_slots
            wait_x(i)
            lane = lax.broadcasted_iota(jnp.int32, (nrows, LANES), 1)
            logits = _dot(xe[cur, r0:r0 + nrows, :], wr_ref[...])
            logits = jnp.where(lane < N_EXPERTS, logits, NEG_INF)
            ex = jnp.exp(logits - jnp.max(logits, axis=-1, keepdims=True))
            gval = (jnp.sum(jnp.where(lane == e, ex, 0.0), axis=-1, keepdims=True)
                    / jnp.sum(ex, axis=-1, keepdims=True))
            for k in range(TILE_ROWS):
                cols = slice(k * LANES, (k + 1) * LANES)
                ix = staged(xst, slot, 0, nrows, k)
                xst[ix] = xst[ix] + gt2_ref[gt_row:gt_row + 1, cols] * (acc[r0:r0 + nrows, cols] * gval)
            issue_rows(idx_ref, x_hbm[src], r0, nrows, xst, slot, 0, sem_x, False, inline)

        gather_x(0, False)
        for i, (r0, nrows, _, _) in enumerate(blocks):
            if i >= 1:
                update_and_scatter(i - 1, True)
            if i < nb:
                prefetch_lat(i)
            if i >= 2:
                wait_x(i - 2)
            if i + 1 <= last:
                gather_x(i + 1, True)
            ffn_rows(r0, nrows)
        update_and_scatter(last, False)
        if last >= 1:
            wait_x(last - 1)
        wait_x(last)

        @pl.when(e == ne - 1)
        def _drain():
            wait_tiles(h2_hbm[0], hst, (nf - 1) % 2, lat_step, sem_h)


def _expert_choice(groups, layer, w_gate, w_up, w_down, wr_pad, gt2):
    d = D_MODEL
    ff = w_gate.shape[-1]
    tf = FF_TILE
    nf = ff // tf
    idx_parts, h2s, xs, counts, caps = [], [], [], [], []
    for h2, aff, x, _ in groups:
        b, n = x.shape[0], _num_tokens(x.shape)
        cap = CAPACITY_FACTOR * n // N_EXPERTS
        idx = _select_call(aff, cap)
        flat = idx + (jnp.arange(b, dtype=jnp.int32) * n)[:, None, None]
        idx_parts.append(jnp.transpose(flat, (1, 0, 2)).reshape(N_EXPERTS, b * cap))
        counts.append(b * cap)
        caps.append(cap)
        h2s.append(h2.reshape(b * n * TILE_ROWS, LANES))
        xs.append(x.reshape(b * n * TILE_ROWS, LANES))
    nsrc = len(groups)
    r_lat = counts[0]
    r_ctx = counts[1] if nsrc > 1 else 0
    rows = r_lat + r_ctx
    rblk = min(ROW_BLOCK, caps[0])
    idx_all = jnp.concatenate(idx_parts, axis=1).reshape(N_EXPERTS, 1, rows)
    stage_tiles = r_lat // nf + r_ctx
    any_spec = pl.BlockSpec(memory_space=pl.ANY)
    idx_spec = lambda shift: pl.BlockSpec(
        (1, 1, rows), lambda ei, fi: (jnp.minimum(ei + shift, N_EXPERTS - 1), 0, 0), memory_space=pltpu.SMEM)
    outs = pl.pallas_call(
        functools.partial(_expert_kernel, nsrc=nsrc, nf=nf, ne=N_EXPERTS, r_lat=r_lat, r_ctx=r_ctx,
                          cap=caps[0], rblk=rblk, gt_rows=tuple(tuple(g[3]) for g in groups)),
        grid=(N_EXPERTS, nf),
        in_specs=[idx_spec(0), idx_spec(1)]
        + [any_spec] * (2 * nsrc)
        + [pl.BlockSpec((1, 1, d, tf), lambda ei, fi: (layer, ei, 0, fi)),
           pl.BlockSpec((1, 1, d, tf), lambda ei, fi: (layer, ei, 0, fi)),
           pl.BlockSpec((1, 1, tf, d), lambda ei, fi: (layer, ei, fi, 0)),
           pl.BlockSpec((d, LANES), lambda ei, fi: (0, 0)),
           pl.BlockSpec(gt2.shape, lambda ei, fi: (0, 0))],
        out_specs=[any_spec] * nsrc,
        out_shape=[jax.ShapeDtypeStruct(xa.shape, F32) for xa in xs],
        scratch_shapes=[pltpu.VMEM((2, rows, d), BF16), pltpu.VMEM((rows, d), F32),
                        pltpu.VMEM((2, stage_tiles * TILE_ROWS, LANES), F32),
                        pltpu.VMEM((X_SLOTS, rblk * TILE_ROWS, LANES), F32),
                        pltpu.SemaphoreType.DMA((2,)), pltpu.SemaphoreType.DMA((X_SLOTS,))],
        input_output_aliases={2 + nsrc + s: s for s in range(nsrc)},
        compiler_params=_cparams(("arbitrary", "arbitrary")),
        name="expert_ffn",
    )(idx_all, idx_all, *h2s, *xs, w_gate, w_up, w_down, wr_pad, gt2)
    return [o.reshape(g[2].shape) for o, g in zip(outs, groups)]


def _final_norm_kernel(x_ref, g_ref, o_ref):
    x = _load_tokens(x_ref)
    ms = jnp.mean(x * x, axis=-1, keepdims=True)
    o_ref[0] = x * lax.rsqrt(ms + EPS) * g_ref[...]


def _final_norm_call(x, g):
    b, n = x.shape[0], _num_tokens(x.shape)
    d = D_MODEL
    tm = min(TOKEN_TILE, n)
    return pl.pallas_call(
        _final_norm_kernel,
        grid=(b, n // tm),
        in_specs=[_token_spec(tm, x.shape),
                  pl.BlockSpec((1, d), lambda bi, i: (0, 0))],
        out_specs=pl.BlockSpec((1, tm, d), lambda bi, i: (bi, i, 0)),
        out_shape=jax.ShapeDtypeStruct((b, n, d), F32),
        compiler_params=_cparams(("parallel", "arbitrary")),
        name="final_norm",
    )(x, g)


def _rope_tables(n):
    rows = n // GRID_W
    row = np.repeat(np.arange(rows, dtype=np.float32), GRID_W)
    col = np.tile(np.arange(GRID_W, dtype=np.float32), rows)
    pairs = HEAD_DIM // 4
    inv = (np.float32(ROPE_BASE) ** (-np.arange(pairs, dtype=np.float32) / pairs)).astype(np.float32)
    ang = np.concatenate([row[:, None] * inv, col[:, None] * inv], axis=-1).astype(np.float32)
    cos, sin = np.cos(ang), np.sin(ang)
    cos_t = np.concatenate([cos, cos, cos, cos], axis=-1)
    sin_t = np.concatenate([-sin, sin, -sin, sin], axis=-1)
    return jnp.asarray(cos_t, F32), jnp.asarray(sin_t, F32)


def _band_bias(l_ctx):
    qi = np.arange(BLOCK)[:, None]
    kj = np.arange(3 * BLOCK)[None, :]
    ok = np.abs(kj - BLOCK - qi) <= WINDOW
    bias = np.concatenate([np.zeros((BLOCK, l_ctx)), np.where(ok, 0.0, NEG_INF)], axis=1).astype(np.float32)
    return jnp.asarray(np.concatenate([bias, bias], axis=0))


def kernel(x, c, ctx, c_ctx, w_mod, b_mod, g_mix, g_ffn, w_in, attn_sink, w_spatial, b_spatial,
           w_branch_attn, w_branch_sgu, w_branch_fourier, w_out, w_router, w_gate, w_up, w_down, g_final):
    b, n, d = x.shape
    l_ctx = ctx.shape[1]
    depth = w_mod.shape[0]

    cvec = jnp.zeros((SUBLANES, d), F32).at[0:b].set(c).at[b].set(c_ctx)
    mod = _mod_call(cvec, w_mod, b_mod).reshape(depth, SUBLANES, N_MOD, d)
    ctx_row = b

    cos_t, sin_t = _rope_tables(n)
    bias = _band_bias(l_ctx)
    no_rope = jnp.zeros((l_ctx, LANES), F32)
    gsz = SGU_WIDTH // SGU_GROUPS
    bd = jnp.asarray(np.kron(np.eye(SGU_GROUPS), np.ones((gsz, gsz))) / gsz, BF16)

    for layer in range(depth):
        last = layer == depth - 1
        w_l = w_in[layer].astype(BF16)
        g1 = g_mix[layer].reshape(1, d)
        g2 = g_ffn[layer].reshape(1, d)
        wsp = jnp.transpose(w_spatial[layer], (1, 0, 2)).reshape(CHUNK, SGU_GROUPS * CHUNK).astype(BF16)
        bsp = jnp.repeat(jnp.transpose(b_spatial[layer]), gsz, axis=1)
        wba = w_branch_attn[layer].astype(BF16)
        wbs = w_branch_sgu[layer].astype(BF16)
        wbf = w_branch_fourier[layer].astype(BF16)
        wo = w_out[layer].astype(BF16)
        wr_t = jnp.transpose(w_router[layer]).astype(BF16)
        wr_pad = jnp.pad(w_router[layer], ((0, 0), (0, LANES - N_EXPERTS))).astype(BF16)
        sink = attn_sink[layer]
        mod_l = mod[layer]

        qc, kcd, vcd, uc, vsc, fc, gc = _proj_call(ctx, mod_l, ctx_row, g1, w_l, no_rope, no_rope, rope=False)
        if not last:
            a_c = _attn_call(qc, None, None, kcd, vcd, sink, None, local=False)
            f_c = _fourier_call(fc)
            ctx_mid, h2c, aff_c = _merge_call(a_c, uc, vsc, f_c, gc, ctx, mod_l, ctx_row, g2,
                                              wsp, bsp, bd, wba, wbs, wbf, wo, wr_t)

        q, kd, vd, u, vs, f, gates = _proj_call(x, mod_l, None, g1, w_l, cos_t, sin_t, rope=True)
        a = _attn_call(q, kd, vd, kcd, vcd, sink, bias, local=True)
        fo = _fourier_call(f)
        x_mid, h2, aff = _merge_call(a, u, vs, fo, gates, x, mod_l, None, g2,
                                     wsp, bsp, bd, wba, wbs, wbf, wo, wr_t)

        gt2 = mod_l[:, 5]
        groups = [(h2, aff, x_mid, list(range(b)))]
        if not last:
            groups.append((h2c, aff_c, ctx_mid, [ctx_row] * b))
        outs = _expert_choice(groups, layer, w_gate, w_up, w_down, wr_pad, gt2)
        x = outs[0]
        if not last:
            ctx = outs[1]
    return _final_norm_call(x, g_final.reshape(1, d))
```
